```python
import math
import jax
import jax.numpy as jnp
from jax import lax
import numpy as np

D_MODEL = 1024
BATCH = 2
SEQ = 16384
DEPTH = 2

N_A_LAYERS = DEPTH // 2
N_B_LAYERS = DEPTH - N_A_LAYERS
MIX_WIDTH = D_MODEL
HEAD_DIM = 64
MEM_HEADS = 4
MEM_WIDTH = MEM_HEADS * HEAD_DIM
SELF_WIDTH = MIX_WIDTH - MEM_WIDTH
S5_GROUP = 16
S5_GROUPS = SELF_WIDTH // S5_GROUP
S5_STATE = 64
S5_CHUNK = 128
S5_DT_MIN = 0.001
S5_DT_MAX = 0.1
MOBA_HEADS = SELF_WIDTH // HEAD_DIM
MOBA_BLOCK = 256
MOBA_TOPK = 3
MOBA_QBLOCK = 64
N_MEM = 256
D_FF = 2816
CONV_WIDTH = 3
ROPE_THETA = 10000.0
NORM_EPS = 1e-6
NEG_INF = -1e30

kernel_name = 'yoco_s5_moba_memory_convffn'


def rms_norm(x, gain):
    xf = x.astype(jnp.float32)
    y = xf * lax.rsqrt(jnp.mean(xf * xf, axis=-1, keepdims=True) + NORM_EPS)
    return (y * gain.astype(jnp.float32)).astype(x.dtype)


def rope_tables(seq):
    pos = jnp.arange(seq, dtype=jnp.float32)
    inv = ROPE_THETA ** (-jnp.arange(0, HEAD_DIM, 2, dtype=jnp.float32) / HEAD_DIM)
    ang = pos[:, None] * inv[None, :]
    return jnp.cos(ang), jnp.sin(ang)


def apply_rope(t, cos, sin):
    half = HEAD_DIM // 2
    tf = t.astype(jnp.float32)
    t1, t2 = tf[..., :half], tf[..., half:]
    c, s = cos[:, None, :], sin[:, None, :]
    return jnp.concatenate([t1 * c - t2 * s, t1 * s + t2 * c], axis=-1).astype(t.dtype)


def _cplx_combine(e1, e2):
    a1r, a1i, b1r, b1i = e1
    a2r, a2i, b2r, b2i = e2
    ar = a2r * a1r - a2i * a1i
    ai = a2r * a1i + a2i * a1r
    br = a2r * b1r - a2i * b1i + b2r
    bi = a2r * b1i + a2i * b1r + b2i
    return (ar, ai, br, bi)


def s5_mixer(u, lam_re, lam_im, log_step, b_re, b_im, c_re, c_im, d_skip, w_glu):
    bsz, seq, _ = u.shape
    f32 = jnp.float32
    uf = u.astype(f32).reshape(bsz, seq, S5_GROUPS, S5_GROUP)
    dt = jnp.exp(log_step.astype(f32))[:, None]
    lr, li = lam_re.astype(f32), lam_im.astype(f32)
    mag = jnp.exp(lr * dt)
    ab_re, ab_im = mag * jnp.cos(li * dt), mag * jnp.sin(li * dt)
    den = lr * lr + li * li
    num_re, num_im = ab_re - 1.0, ab_im
    coef_re = (num_re * lr + num_im * li) / den
    coef_im = (num_im * lr - num_re * li) / den
    br, bi = b_re.astype(f32), b_im.astype(f32)
    bb_re = coef_re[..., None] * br - coef_im[..., None] * bi
    bb_im = coef_re[..., None] * bi + coef_im[..., None] * br
    cr, ci = c_re.astype(f32), c_im.astype(f32)
    dd = d_skip.astype(f32)
    n_chunks = seq // S5_CHUNK
    u_chunks = uf.reshape(bsz, n_chunks, S5_CHUNK, S5_GROUPS, S5_GROUP).transpose(1, 0, 2, 3, 4)

    def step(carry, u_c):
        h_re, h_im = carry
        bu_re = jnp.einsum('blgp,gnp->blgn', u_c, bb_re)
        bu_im = jnp.einsum('blgp,gnp->blgn', u_c, bb_im)
        a_re = jnp.broadcast_to(ab_re, bu_re.shape)
        a_im = jnp.broadcast_to(ab_im, bu_im.shape)
        acc_re, acc_im, x_re, x_im = lax.associative_scan(
            _cplx_combine, (a_re, a_im, bu_re, bu_im), axis=1)
        s_re = x_re + acc_re * h_re[:, None] - acc_im * h_im[:, None]
        s_im = x_im + acc_re * h_im[:, None] + acc_im * h_re[:, None]
        y = (jnp.einsum('gpn,blgn->blgp', cr, s_re)
             - jnp.einsum('gpn,blgn->blgp', ci, s_im) + dd * u_c)
        return (s_re[:, -1], s_im[:, -1]), y

    init = (jnp.zeros((bsz, S5_GROUPS, S5_STATE), f32),
            jnp.zeros((bsz, S5_GROUPS, S5_STATE), f32))
    _, ys = lax.scan(step, init, u_chunks)
    y = ys.transpose(1, 0, 2, 3, 4).reshape(bsz, seq, SELF_WIDTH)
    y = jax.nn.gelu(y)
    y = y * jax.nn.sigmoid(y @ w_glu.astype(f32))
    return y.astype(u.dtype)


def make_shared_kv(x, kv_norm, w_kv, cos, sin):
    bsz, seq, _ = x.shape
    h = rms_norm(x, kv_norm)
    kv = h @ w_kv
    k = kv[..., :SELF_WIDTH].reshape(bsz, seq, MOBA_HEADS, HEAD_DIM)
    v = kv[..., SELF_WIDTH:].reshape(bsz, seq, MOBA_HEADS, HEAD_DIM)
    k = apply_rope(k, cos, sin)
    n_blocks = max(-(-seq // MOBA_BLOCK), MOBA_TOPK)
    pad = n_blocks * MOBA_BLOCK - seq
    k = jnp.pad(k, ((0, 0), (0, pad), (0, 0), (0, 0)))
    v = jnp.pad(v, ((0, 0), (0, pad), (0, 0), (0, 0)))
    k_blocks = k.reshape(bsz, n_blocks, MOBA_BLOCK, MOBA_HEADS, HEAD_DIM).transpose(0, 3, 1, 2, 4)
    v_blocks = v.reshape(bsz, n_blocks, MOBA_BLOCK, MOBA_HEADS, HEAD_DIM).transpose(0, 3, 1, 2, 4)
    k_mean = jnp.mean(k_blocks.astype(jnp.float32), axis=3)
    return k_blocks, v_blocks, k_mean


def moba_attention(q, k_blocks, v_blocks, k_mean):
    bsz, seq = q.shape[0], q.shape[1]
    n_blocks = k_blocks.shape[2]
    n_qb = seq // MOBA_QBLOCK
    scale = HEAD_DIM ** -0.5
    q_blocks = q.reshape(bsz, n_qb, MOBA_QBLOCK, MOBA_HEADS, HEAD_DIM).transpose(1, 0, 2, 3, 4)
    b_ix = jnp.arange(bsz)[:, None, None, None]
    h_ix = jnp.arange(MOBA_HEADS)[None, :, None, None]
    n_sel = MOBA_TOPK * MOBA_BLOCK

    def one_block(args):
        qi, qb = args
        q_start = qi * MOBA_QBLOCK
        blk = q_start // MOBA_BLOCK
        gate = jnp.einsum('bqhd,bhnd->bhqn', qb.astype(jnp.float32), k_mean)
        past = jnp.arange(n_blocks) < blk
        gate = jnp.where(past[None, None, None, :], gate, -jnp.inf)
        _, sel = lax.top_k(gate, MOBA_TOPK)
        sel_valid = sel < blk
        kg = k_blocks[b_ix, h_ix, sel]
        vg = v_blocks[b_ix, h_ix, sel]
        s_sel = jnp.einsum('bqhd,bhqtkd->bhqtk', qb, kg,
                           preferred_element_type=jnp.float32) * scale
        s_sel = jnp.where(sel_valid[..., None], s_sel, NEG_INF)
        s_sel = s_sel.reshape(bsz, MOBA_HEADS, MOBA_QBLOCK, n_sel)
        k_own = lax.dynamic_index_in_dim(k_blocks, blk, axis=2, keepdims=False)
        v_own = lax.dynamic_index_in_dim(v_blocks, blk, axis=2, keepdims=False)
        s_own = jnp.einsum('bqhd,bhkd->bhqk', qb, k_own,
                           preferred_element_type=jnp.float32) * scale
        q_pos = q_start + jnp.arange(MOBA_QBLOCK)
        k_pos = blk * MOBA_BLOCK + jnp.arange(MOBA_BLOCK)
        s_own = jnp.where(k_pos[None, :] <= q_pos[:, None], s_own, NEG_INF)
        p = jax.nn.softmax(jnp.concatenate([s_sel, s_own], axis=-1), axis=-1)
        p_sel = p[..., :n_sel].reshape(bsz, MOBA_HEADS, MOBA_QBLOCK, MOBA_TOPK, MOBA_BLOCK).astype(vg.dtype)
        p_own = p[..., n_sel:].astype(v_own.dtype)
        return (jnp.einsum('bhqtk,bhqtkd->bqhd', p_sel, vg)
                + jnp.einsum('bhqk,bhkd->bqhd', p_own, v_own))

    out = lax.map(one_block, (jnp.arange(n_qb), q_blocks))
    return out.transpose(1, 0, 2, 3, 4).reshape(bsz, seq, SELF_WIDTH)


def memory_attention(q_mem, mem, mem_gain, w_mem_kv):
    bsz, seq, _ = q_mem.shape
    m = mem.shape[1]
    kv = rms_norm(mem, mem_gain) @ w_mem_kv
    k = kv[..., :MEM_WIDTH].reshape(bsz, m, MEM_HEADS, HEAD_DIM)
    v = kv[..., MEM_WIDTH:].reshape(bsz, m, MEM_HEADS, HEAD_DIM)
    q = q_mem.reshape(bsz, seq, MEM_HEADS, HEAD_DIM)
    s = jnp.einsum('bshd,bmhd->bhsm', q, k, preferred_element_type=jnp.float32) * HEAD_DIM ** -0.5
    p = jax.nn.softmax(s, axis=-1).astype(v.dtype)
    return jnp.einsum('bhsm,bmhd->bshd', p, v).reshape(bsz, seq, MEM_WIDTH)


def causal_dwconv(x, w, b):
    ch = x.shape[-1]
    y = lax.conv_general_dilated(
        x, w[:, None, :].astype(x.dtype), window_strides=(1,),
        padding=[(CONV_WIDTH - 1, 0)], dimension_numbers=('NWC', 'WIO', 'NWC'),
        feature_group_count=ch)
    return y + b.astype(x.dtype)


def conv_ffn(x, gain, w_up, conv_w, conv_b, w_down):
    h = rms_norm(x, gain)
    up = causal_dwconv(h @ w_up, conv_w, conv_b)
    g, v = up[..., :D_FF], up[..., D_FF:]
    return (jax.nn.silu(g) * v) @ w_down


def setup_inputs(seed: int = 0) -> dict:
    key = jax.random.key(seed)
    ks = jax.random.split(key, 32)
    f32 = jnp.float32

    def nrm(k, shape, scale):
        return jax.random.normal(k, shape, f32) * scale

    def gain(k, shape):
        return 1.0 + 0.05 * jax.random.normal(k, shape, f32)

    G, N, P = S5_GROUPS, S5_STATE, S5_GROUP
    lam_re = -0.5 + 0.01 * jax.random.normal(ks[12], (N_A_LAYERS, G, N), f32)
    lam_im = (jnp.pi * jnp.arange(N, dtype=f32))[None, None, :] + 0.01 * jax.random.normal(ks[13], (N_A_LAYERS, G, N), f32)
    log_step = math.log(S5_DT_MIN) + jax.random.uniform(ks[14], (N_A_LAYERS, G), f32) * (math.log(S5_DT_MAX) - math.log(S5_DT_MIN))
    return {
        'x': jax.random.normal(ks[0], (BATCH, SEQ, D_MODEL), f32),
        'mem': jax.random.normal(ks[1], (BATCH, N_MEM, D_MODEL), f32),
        'ln_mix': gain(ks[2], (DEPTH, D_MODEL)),
        'w_in': nrm(ks[3], (DEPTH, D_MODEL, MIX_WIDTH), D_MODEL ** -0.5),
        'w_out': nrm(ks[4], (DEPTH, MIX_WIDTH, D_MODEL), MIX_WIDTH ** -0.5),
        'mem_norm': gain(ks[5], (DEPTH, D_MODEL)),
        'w_mem_kv': nrm(ks[6], (DEPTH, D_MODEL, 2 * MEM_WIDTH), D_MODEL ** -0.5),
        'ln_ffn': gain(ks[7], (DEPTH, D_MODEL)),
        'w_up': nrm(ks[8], (DEPTH, D_MODEL, 2 * D_FF), D_MODEL ** -0.5),
        'conv_w': nrm(ks[9], (DEPTH, CONV_WIDTH, 2 * D_FF), CONV_WIDTH ** -0.5),
        'conv_b': nrm(ks[10], (DEPTH, 2 * D_FF), 0.01),
        'w_down': nrm(ks[11], (DEPTH, D_FF, D_MODEL), D_FF ** -0.5),
        's5_lambda_re': lam_re,
        's5_lambda_im': lam_im,
        's5_log_step': log_step,
        's5_b_re': nrm(ks[15], (N_A_LAYERS, G, N, P), (2 * P) ** -0.5),
        's5_b_im': nrm(ks[16], (N_A_LAYERS, G, N, P), (2 * P) ** -0.5),
        's5_c_re': nrm(ks[17], (N_A_LAYERS, G, P, N), (2 * N) ** -0.5),
        's5_c_im': nrm(ks[18], (N_A_LAYERS, G, P, N), (2 * N) ** -0.5),
        's5_d': nrm(ks[19], (N_A_LAYERS, G, P), 1.0),
        's5_w_glu': nrm(ks[20], (N_A_LAYERS, SELF_WIDTH, SELF_WIDTH), SELF_WIDTH ** -0.5),
        'kv_norm': gain(ks[21], (D_MODEL,)),
        'w_kv': nrm(ks[22], (D_MODEL, 2 * SELF_WIDTH), D_MODEL ** -0.5),
        'final_norm': gain(ks[23], (D_MODEL,)),
    }


def reference(x, mem, ln_mix, w_in, w_out, mem_norm, w_mem_kv, ln_ffn, w_up, conv_w, conv_b,
              w_down, s5_lambda_re, s5_lambda_im, s5_log_step, s5_b_re, s5_b_im, s5_c_re,
              s5_c_im, s5_d, s5_w_glu, kv_norm, w_kv, final_norm):
    bsz, seq, _ = x.shape
    cos, sin = rope_tables(seq)
    shared = None
    for l in range(DEPTH):
        h = rms_norm(x, ln_mix[l])
        z = h @ w_in[l]
        u_self, q_mem = z[..., :SELF_WIDTH], z[..., SELF_WIDTH:]
        if l < N_A_LAYERS:
            self_out = s5_mixer(u_self, s5_lambda_re[l], s5_lambda_im[l], s5_log_step[l],
                                s5_b_re[l], s5_b_im[l], s5_c_re[l], s5_c_im[l], s5_d[l], s5_w_glu[l])
        else:
            if l == N_A_LAYERS:
                shared = make_shared_kv(x, kv_norm, w_kv, cos, sin)
            q = apply_rope(u_self.reshape(bsz, seq, MOBA_HEADS, HEAD_DIM), cos, sin)
            self_out = moba_attention(q, shared[0], shared[1], shared[2])
        mem_out = memory_attention(q_mem, mem, mem_norm[l], w_mem_kv[l])
        x = x + jnp.concatenate([self_out, mem_out], axis=-1) @ w_out[l]
        x = x + conv_ffn(x, ln_ffn[l], w_up[l], conv_w[l], conv_b[l], w_down[l])
    return rms_norm(x, final_norm)
```

```python
import functools
import math

import jax
import jax.numpy as jnp
from jax import lax
from jax.experimental import pallas as pl
from jax.experimental.pallas import tpu as pltpu

F32 = jnp.float32
BF16 = jnp.bfloat16

D_MODEL = 1024
HEAD_DIM = 64
MEM_HEADS = 4
MEM_WIDTH = MEM_HEADS * HEAD_DIM
SELF_WIDTH = D_MODEL - MEM_WIDTH
S5_GROUP = 16
S5_GROUPS = SELF_WIDTH // S5_GROUP
S5_STATE = 64
MOBA_BLOCK = 256
MOBA_TOPK = 3
D_FF = 2816
CONV_WIDTH = 3
ROPE_THETA = 10000.0
NORM_EPS = 1e-6
NEG_INF = -1e30
ATTN_SCALE = HEAD_DIM ** -0.5

LANES = 128
HEAD_PAIRS = SELF_WIDTH // LANES
S5_CHUNK = 8
S5_LANE_BLOCKS = SELF_WIDTH // LANES
S5_GROUPS_PER_BLOCK = LANES // S5_GROUP
S5_BLOCK_STATE = S5_GROUPS_PER_BLOCK * S5_STATE
S5_ROWS = 256
FFN_CHUNK = 256
FFN_HALO = 16
VMEM_LIMIT = 56 * 1024 * 1024


def _params(sem):
    return pltpu.CompilerParams(dimension_semantics=sem, vmem_limit_bytes=VMEM_LIMIT)


def _rms(x, gain):
    ms = jnp.mean(x * x, axis=-1, keepdims=True)
    return x * lax.rsqrt(ms + NORM_EPS) * gain


def _dot(a, b):
    return jnp.dot(a, b, preferred_element_type=F32)


def _dot_nt(a, b):
    return lax.dot_general(a, b, (((1,), (1,)), ((), ())), preferred_element_type=F32)


def _norm_matmul_kernel(x_ref, g_ref, w_ref, o_ref):
    h = _rms(x_ref[...], g_ref[...]).astype(BF16)
    o_ref[...] = _dot(h, w_ref[...]).astype(o_ref.dtype)


def norm_matmul(x, gain, w, tile, out_dtype):
    rows, d = x.shape
    n = w.shape[1]
    return pl.pallas_call(
        _norm_matmul_kernel,
        grid=(rows // tile,),
        in_specs=[pl.BlockSpec((tile, d), lambda i: (i, 0)),
                  pl.BlockSpec((1, d), lambda i: (0, 0)),
                  pl.BlockSpec((d, n), lambda i: (0, 0))],
        out_specs=pl.BlockSpec((tile, n), lambda i: (i, 0)),
        out_shape=jax.ShapeDtypeStruct((rows, n), out_dtype),
        compiler_params=_params(("parallel",)),
        name="norm_matmul",
    )(x, gain.reshape(1, d), w)


def _inproj_s5_kernel(x_ref, g_ref, w_ref, u_ref, qm_ref):
    h = _rms(x_ref[...], g_ref[...]).astype(BF16)
    z = _dot(h, w_ref[...])
    u_ref[...] = z[:, :SELF_WIDTH]
    qm_ref[...] = z[:, SELF_WIDTH:].astype(BF16)


def inproj_s5(x, gain, w, tile=512):
    rows, d = x.shape
    return pl.pallas_call(
        _inproj_s5_kernel,
        grid=(rows // tile,),
        in_specs=[pl.BlockSpec((tile, d), lambda i: (i, 0)),
                  pl.BlockSpec((1, d), lambda i: (0, 0)),
                  pl.BlockSpec((d, d), lambda i: (0, 0))],
        out_specs=[pl.BlockSpec((tile, SELF_WIDTH), lambda i: (i, 0)),
                   pl.BlockSpec((tile, MEM_WIDTH), lambda i: (i, 0))],
        out_shape=[jax.ShapeDtypeStruct((rows, SELF_WIDTH), F32),
                   jax.ShapeDtypeStruct((rows, MEM_WIDTH), BF16)],
        compiler_params=_params(("parallel",)),
        name="inproj_s5",
    )(x, gain.reshape(1, d), w)


def _rope(z, cos, sin_signed):
    lane = lax.broadcasted_iota(jnp.int32, (z.shape[0], LANES), 1)
    first_half = (lane % HEAD_DIM) < (HEAD_DIM // 2)
    outs = []
    for j in range(z.shape[1] // LANES):
        zj = z[:, j * LANES:(j + 1) * LANES]
        partner = jnp.where(first_half,
                            pltpu.roll(zj, LANES - HEAD_DIM // 2, axis=1),
                            pltpu.roll(zj, HEAD_DIM // 2, axis=1))
        outs.append(zj * cos + partner * sin_signed)
    return jnp.concatenate(outs, axis=1)


def _inproj_moba_kernel(x_ref, gq_ref, gkv_ref, wq_ref, wkv_ref, cos_ref, sin_ref,
                        q_ref, qm_ref, k_ref, vt_ref, km_ref):
    x = x_ref[...]
    xn = x * lax.rsqrt(jnp.mean(x * x, axis=-1, keepdims=True) + NORM_EPS)
    cos, sin = cos_ref[...], sin_ref[...]
    z = _dot((xn * gq_ref[...]).astype(BF16), wq_ref[...])
    q_ref[...] = _rope(z[:, :SELF_WIDTH], cos, sin).astype(BF16)
    qm_ref[...] = z[:, SELF_WIDTH:].astype(BF16)
    kv = _dot((xn * gkv_ref[...]).astype(BF16), wkv_ref[...])
    k = _rope(kv[:, :SELF_WIDTH], cos, sin)
    k_ref[0] = k.astype(BF16)
    km_ref[0] = jnp.mean(k, axis=0, keepdims=True)
    vt_ref[0] = kv[:, SELF_WIDTH:].T.astype(BF16)


def inproj_moba(x, gq, gkv, wq, wkv, cos, sin, seq):
    rows, d = x.shape
    tile = MOBA_BLOCK
    nblk = rows // tile
    per_seq = seq // tile
    return pl.pallas_call(
        _inproj_moba_kernel,
        grid=(nblk,),
        in_specs=[pl.BlockSpec((tile, d), lambda i: (i, 0)),
                  pl.BlockSpec((1, d), lambda i: (0, 0)),
                  pl.BlockSpec((1, d), lambda i: (0, 0)),
                  pl.BlockSpec((d, d), lambda i: (0, 0)),
                  pl.BlockSpec((d, 2 * SELF_WIDTH), lambda i: (0, 0)),
                  pl.BlockSpec((tile, LANES), lambda i: (i % per_seq, 0)),
                  pl.BlockSpec((tile, LANES), lambda i: (i % per_seq, 0))],
        out_specs=[pl.BlockSpec((tile, SELF_WIDTH), lambda i: (i, 0)),
                   pl.BlockSpec((tile, MEM_WIDTH), lambda i: (i, 0)),
                   pl.BlockSpec((1, tile, SELF_WIDTH), lambda i: (i, 0, 0)),
                   pl.BlockSpec((1, SELF_WIDTH, tile), lambda i: (i, 0, 0)),
                   pl.BlockSpec((1, 1, SELF_WIDTH), lambda i: (i, 0, 0))],
        out_shape=[jax.ShapeDtypeStruct((rows, SELF_WIDTH), BF16),
                   jax.ShapeDtypeStruct((rows, MEM_WIDTH), BF16),
                   jax.ShapeDtypeStruct((nblk, tile, SELF_WIDTH), BF16),
                   jax.ShapeDtypeStruct((nblk, SELF_WIDTH, tile), BF16),
                   jax.ShapeDtypeStruct((nblk, 1, SELF_WIDTH), F32)],
        compiler_params=_params(("parallel",)),
        name="inproj_moba",
    )(x, gq.reshape(1, d), gkv.reshape(1, d), wq, wkv, cos, sin)


def _s5_weights(lam_re, lam_im, log_step, b_re, b_im, c_re, c_im, d_skip):
    L = S5_CHUNK
    G, N, P = S5_GROUPS, S5_STATE, S5_GROUP
    JB, GL = S5_LANE_BLOCKS, S5_GROUPS_PER_BLOCK
    dt = jnp.exp(log_step)[:, None]
    mag = jnp.exp(lam_re * dt)
    ab_re, ab_im = mag * jnp.cos(lam_im * dt), mag * jnp.sin(lam_im * dt)
    den = lam_re * lam_re + lam_im * lam_im
    num_re, num_im = ab_re - 1.0, ab_im
    coef_re = (num_re * lam_re + num_im * lam_im) / den
    coef_im = (num_im * lam_re - num_re * lam_im) / den
    bb_re = coef_re[..., None] * b_re - coef_im[..., None] * b_im
    bb_im = coef_re[..., None] * b_im + coef_im[..., None] * b_re

    def power(j):
        j = jnp.asarray(j, F32).reshape((-1, 1, 1))
        m = jnp.exp(j * (lam_re * dt))
        return m * jnp.cos(j * (lam_im * dt)), m * jnp.sin(j * (lam_im * dt))

    pw_re, pw_im = power(jnp.arange(L + 1))
    cb_re = (jnp.einsum('gpn,dgn,gnq->dgpq', c_re, pw_re, bb_re)
             - jnp.einsum('gpn,dgn,gnq->dgpq', c_re, pw_im, bb_im)
             - jnp.einsum('gpn,dgn,gnq->dgpq', c_im, pw_re, bb_im)
             - jnp.einsum('gpn,dgn,gnq->dgpq', c_im, pw_im, bb_re))
    cb_re = cb_re.at[0].add(jnp.einsum('gp,pq->gpq', d_skip, jnp.eye(P, dtype=F32)))
    s_ix = jnp.arange(L)[:, None]
    t_ix = jnp.arange(L)[None, :]
    delta = jnp.clip(t_ix - s_ix, 0, L - 1)
    kst = jnp.where((t_ix >= s_ix)[..., None, None, None], cb_re[delta], 0.0)
    kst = kst.reshape(L, L, JB, GL, P, P)
    eye = jnp.eye(GL, dtype=F32)
    intra = jnp.einsum('stjgpq,gh->jsgqthp', kst, eye).reshape(JB, L * LANES, L * LANES)

    rev_re, rev_im = pw_re[L - 1 - jnp.arange(L)], pw_im[L - 1 - jnp.arange(L)]
    inj_re = rev_re[..., None] * bb_re[None] - rev_im[..., None] * bb_im[None]
    inj_im = rev_re[..., None] * bb_im[None] + rev_im[..., None] * bb_re[None]
    inj = jnp.stack([inj_re, inj_im], axis=0).reshape(2, L, JB, GL, N, P)
    inject = jnp.einsum('csjgnp,gh->jsgpchn', inj, eye).reshape(JB, L * LANES, 2 * S5_BLOCK_STATE)

    nx_re, nx_im = pw_re[1:], pw_im[1:]
    ro_re = c_re[None] * nx_re[:, :, None, :] - c_im[None] * nx_im[:, :, None, :]
    ro_im = -(c_re[None] * nx_im[:, :, None, :] + c_im[None] * nx_re[:, :, None, :])
    ro = jnp.stack([ro_re, ro_im], axis=0).reshape(2, L, JB, GL, P, N)
    readout = jnp.einsum('ctjgpn,gh->jcgnthp', ro, eye).reshape(JB, 2 * S5_BLOCK_STATE, L * LANES)

    levels = int(math.log2(S5_ROWS))
    sc_re, sc_im = power(L * (2 ** jnp.arange(levels)))
    scan = jnp.stack([sc_re, sc_im], axis=1).reshape(levels, 2, JB, GL * N)
    scan = scan.transpose(2, 0, 1, 3).reshape(JB, levels, 2 * S5_BLOCK_STATE)
    return intra.astype(BF16), inject.astype(BF16), readout.astype(BF16), scan


def _s5_kernel(u_ref, intra_ref, inject_ref, readout_ref, scan_ref, y_ref, h_ref):
    rows = u_ref.shape[0]
    ns = S5_BLOCK_STATE

    @pl.when(pl.program_id(2) == 0)
    def _():
        h_ref[...] = jnp.zeros_like(h_ref)

    x = jnp.concatenate([u_ref[:, t, :] for t in range(S5_CHUNK)], axis=1).astype(BF16)
    e = _dot(x, inject_ref[0])
    er, ei = e[:, :ns], e[:, ns:]
    h_in = h_ref[...]
    hr, hi = h_in[:, :ns], h_in[:, ns:]
    row = lax.broadcasted_iota(jnp.int32, (rows, ns), 0)
    first = row == 0
    a = scan_ref[0]
    ar, ai = a[0:1, :ns], a[0:1, ns:]
    er = er + jnp.where(first, ar * hr - ai * hi, 0.0)
    ei = ei + jnp.where(first, ar * hi + ai * hr, 0.0)
    for k in range(a.shape[0]):
        d = 1 << k
        ar, ai = a[k:k + 1, :ns], a[k:k + 1, ns:]
        keep = row >= d
        sr = jnp.where(keep, pltpu.roll(er, d, axis=0), 0.0)
        si = jnp.where(keep, pltpu.roll(ei, d, axis=0), 0.0)
        er, ei = er + ar * sr - ai * si, ei + ar * si + ai * sr
    h_ref[...] = jnp.concatenate([er[rows - 1:rows], ei[rows - 1:rows]], axis=1)
    hx_r = jnp.where(first, hr, pltpu.roll(er, 1, axis=0))
    hx_i = jnp.where(first, hi, pltpu.roll(ei, 1, axis=0))
    hx = jnp.concatenate([hx_r, hx_i], axis=1).astype(BF16)
    y = _dot(x, intra_ref[0]) + _dot(hx, readout_ref[0])
    for t in range(S5_CHUNK):
        y_ref[:, t, :] = y[:, t * LANES:(t + 1) * LANES]


def s5_ssm(u, weights, batch, seq):
    intra, inject, readout, scan = weights
    rows_per_seq = seq // S5_CHUNK
    tile = min(S5_ROWS, rows_per_seq)
    assert tile == S5_ROWS, "sequence too short for the S5 row tile"
    nrt = rows_per_seq // tile
    u3 = u.reshape(batch * rows_per_seq, S5_CHUNK, SELF_WIDTH)
    wide = S5_CHUNK * LANES
    y3 = pl.pallas_call(
        _s5_kernel,
        grid=(batch, S5_LANE_BLOCKS, nrt),
        in_specs=[pl.BlockSpec((tile, S5_CHUNK, LANES), lambda b, j, r: (b * nrt + r, 0, j)),
                  pl.BlockSpec((1, wide, wide), lambda b, j, r: (j, 0, 0)),
                  pl.BlockSpec((1, wide, 2 * S5_BLOCK_STATE), lambda b, j, r: (j, 0, 0)),
                  pl.BlockSpec((1, 2 * S5_BLOCK_STATE, wide), lambda b, j, r: (j, 0, 0)),
                  pl.BlockSpec((1, scan.shape[1], 2 * S5_BLOCK_STATE), lambda b, j, r: (j, 0, 0))],
        out_specs=pl.BlockSpec((tile, S5_CHUNK, LANES), lambda b, j, r: (b * nrt + r, 0, j)),
        out_shape=jax.ShapeDtypeStruct(u3.shape, F32),
        scratch_shapes=[pltpu.VMEM((1, 2 * S5_BLOCK_STATE), F32)],
        compiler_params=_params(("parallel", "parallel", "arbitrary")),
        name="s5_ssm",
    )(u3, intra, inject, readout, scan)
    return y3.reshape(u.shape)


def _moba_kernel(q_ref, k_ref, vt_ref, kmh_ref, kml_ref, o_ref, bias_ref, m_ref, l_ref, acc_ref):
    j = pl.program_id(2)
    nb = k_ref.shape[0]
    tq = q_ref.shape[0]
    q = q_ref[...]
    lane = lax.broadcasted_iota(jnp.int32, q.shape, 1)
    zero = jnp.zeros_like(q)
    qc = jnp.concatenate([jnp.where(lane < HEAD_DIM, q, zero), jnp.where(lane >= HEAD_DIM, q, zero)], axis=0)

    gate = _dot_nt(kmh_ref[0], qc) + _dot_nt(kml_ref[0], qc)
    blk = lax.broadcasted_iota(jnp.int32, gate.shape, 0).astype(F32)
    past = blk < j.astype(F32)
    g = jnp.where(past, gate, -jnp.inf)
    bias = jnp.full(gate.shape, NEG_INF, F32)
    for _ in range(MOBA_TOPK):
        top = jnp.max(g, axis=0, keepdims=True)
        idx = jnp.min(jnp.where(g == top, blk, float(nb)), axis=0, keepdims=True)
        hit = blk == idx
        bias = jnp.where(hit, jnp.where(past, 0.0, NEG_INF), bias)
        g = jnp.where(hit, -jnp.inf, g)
    bias_ref[...] = bias

    s = _dot_nt(k_ref[j], qc) * ATTN_SCALE
    kpos = lax.broadcasted_iota(jnp.int32, s.shape, 0)
    qpos = lax.broadcasted_iota(jnp.int32, s.shape, 1) % tq
    s = jnp.where(kpos <= qpos, s, NEG_INF)
    m0 = jnp.max(s, axis=0, keepdims=True)
    p = jnp.exp(s - m0)
    m_ref[...] = m0
    l_ref[...] = jnp.sum(p, axis=0, keepdims=True)
    acc_ref[...] = _dot(vt_ref[j], p.astype(BF16))

    def body(n, carry):
        s = _dot_nt(k_ref[n], qc) * ATTN_SCALE + bias_ref[pl.ds(n, 1), :]
        m_old = m_ref[...]
        m_new = jnp.maximum(m_old, jnp.max(s, axis=0, keepdims=True))
        alpha = jnp.exp(m_old - m_new)
        p = jnp.exp(s - m_new)
        l_ref[...] = alpha * l_ref[...] + jnp.sum(p, axis=0, keepdims=True)
        acc_ref[...] = alpha * acc_ref[...] + _dot(vt_ref[n], p.astype(BF16))
        m_ref[...] = m_new
        return carry

    lax.fori_loop(0, j, body, 0)
    o = acc_ref[...] / l_ref[...]
    ot = jnp.concatenate([o[:HEAD_DIM, :tq], o[HEAD_DIM:, tq:]], axis=0)
    o_ref[...] = ot.T.astype(o_ref.dtype)


def moba_attention(q, k3, vt3, kmh, kml, batch, seq):
    rows = q.shape[0]
    nb = seq // MOBA_BLOCK
    tq = MOBA_BLOCK
    return pl.pallas_call(
        _moba_kernel,
        grid=(batch, HEAD_PAIRS, nb),
        in_specs=[pl.BlockSpec((tq, LANES), lambda b, h, j: (b * nb + j, h)),
                  pl.BlockSpec((nb, MOBA_BLOCK, LANES), lambda b, h, j: (b, 0, h)),
                  pl.BlockSpec((nb, LANES, MOBA_BLOCK), lambda b, h, j: (b, h, 0)),
                  pl.BlockSpec((1, nb, LANES), lambda b, h, j: (b, 0, h)),
                  pl.BlockSpec((1, nb, LANES), lambda b, h, j: (b, 0, h))],
        out_specs=pl.BlockSpec((tq, LANES), lambda b, h, j: (b * nb + j, h)),
        out_shape=jax.ShapeDtypeStruct((rows, SELF_WIDTH), BF16),
        scratch_shapes=[pltpu.VMEM((nb, 2 * tq), F32),
                        pltpu.VMEM((1, 2 * tq), F32),
                        pltpu.VMEM((1, 2 * tq), F32),
                        pltpu.VMEM((LANES, 2 * tq), F32)],
        compiler_params=_params(("parallel", "parallel", "arbitrary")),
        name="moba_attention",
    )(q, k3, vt3, kmh, kml)


def _memory_heads(qm, km, vm):
    lane = lax.broadcasted_iota(jnp.int32, (qm.shape[0], LANES), 1)
    lo = lane < HEAD_DIM
    outs = []
    for pair in range(MEM_WIDTH // LANES):
        sl = slice(pair * LANES, (pair + 1) * LANES)
        qp, kp, vp = qm[:, sl], km[:, sl], vm[:, sl]
        zero = jnp.zeros_like(qp)
        res = []
        for keep in (lo, jnp.logical_not(lo)):
            s = _dot_nt(jnp.where(keep, qp, zero), kp) * ATTN_SCALE
            p = jnp.exp(s - jnp.max(s, axis=-1, keepdims=True))
            p = p / jnp.sum(p, axis=-1, keepdims=True)
            res.append(_dot(p.astype(BF16), vp))
        outs.append(jnp.where(lo, res[0], res[1]))
    return jnp.concatenate(outs, axis=1)


def _mixout_kernel(x_ref, s_ref, qm_ref, km_ref, vm_ref, *rest, glu):
    if glu:
        wg_ref, wo_ref, o_ref = rest
        y = jax.nn.gelu(s_ref[...])
        gate = _dot(y.astype(BF16), wg_ref[...])
        self_out = (y / (1.0 + jnp.exp(-gate))).astype(BF16)
    else:
        wo_ref, o_ref = rest
        self_out = s_ref[...]
    mem_out = _memory_heads(qm_ref[...], km_ref[...], vm_ref[...]).astype(BF16)
    mixed = jnp.concatenate([self_out, mem_out], axis=1)
    o_ref[...] = x_ref[...] + _dot(mixed, wo_ref[...])


def mix_out(x, self_pre, qm, mem_kv, w_glu, w_out, seq, tile=512):
    rows, d = x.shape
    n_mem = mem_kv.shape[0] // (rows // seq)
    per_seq = seq // tile
    glu = w_glu is not None
    in_specs = [pl.BlockSpec((tile, d), lambda i: (i, 0)),
                pl.BlockSpec((tile, SELF_WIDTH), lambda i: (i, 0)),
                pl.BlockSpec((tile, MEM_WIDTH), lambda i: (i, 0)),
                pl.BlockSpec((n_mem, MEM_WIDTH), lambda i: (i // per_seq, 0)),
                pl.BlockSpec((n_mem, MEM_WIDTH), lambda i: (i // per_seq, 1))]
    args = [x, self_pre, qm, mem_kv, mem_kv]
    if glu:
        in_specs.append(pl.BlockSpec((SELF_WIDTH, SELF_WIDTH), lambda i: (0, 0)))
        args.append(w_glu)
    in_specs.append(pl.BlockSpec((d, d), lambda i: (0, 0)))
    args.append(w_out)
    return pl.pallas_call(
        functools.partial(_mixout_kernel, glu=glu),
        grid=(rows // tile,),
        in_specs=in_specs,
        out_specs=pl.BlockSpec((tile, d), lambda i: (i, 0)),
        out_shape=jax.ShapeDtypeStruct((rows, d), F32),
        compiler_params=_params(("parallel",)),
        name="mix_out_glu" if glu else "mix_out",
    )(*args)


def _ffn_kernel(x_ref, halo_ref, g_ref, wg_ref, wv_ref, cw_ref, wd_ref, *rest, per_seq, final):
    if final:
        gf_ref, o_ref, acc_ref = rest
    else:
        o_ref, acc_ref = rest
    tile = x_ref.shape[0]
    x = x_ref[...]
    gain = g_ref[...]
    halo = jnp.where(pl.program_id(0) % per_seq == 0, 0.0, halo_ref[...])
    hcat = jnp.concatenate([_rms(halo, gain).astype(BF16), _rms(x, gain).astype(BF16)], axis=0)
    acc_ref[...] = x

    def conv(u, w):
        return (w[0:1] * pltpu.roll(u, 2, axis=0)[FFN_HALO:] + w[1:2] * pltpu.roll(u, 1, axis=0)[FFN_HALO:]
                + w[2:3] * u[FFN_HALO:] + w[3:4])

    def chunk(c, carry):
        cw = cw_ref[c]
        gg = conv(_dot(hcat, wg_ref[c]), cw[0:4])
        vv = conv(_dot(hcat, wv_ref[c]), cw[4:8])
        act = (gg / (1.0 + jnp.exp(-gg)) * vv).astype(BF16)
        acc_ref[...] += _dot(act, wd_ref[c])
        return carry

    lax.fori_loop(0, wg_ref.shape[0], chunk, 0)
    out = acc_ref[...]
    if final:
        out = _rms(out, gf_ref[...])
    o_ref[...] = out


def conv_ffn(x, gain, w_up, conv_w, conv_b, w_down, seq, final_gain=None, tile=512):
    rows, d = x.shape
    nch = D_FF // FFN_CHUNK
    per_seq = seq // tile
    wg = w_up[:, :D_FF].reshape(d, nch, FFN_CHUNK).transpose(1, 0, 2).astype(BF16)
    wv = w_up[:, D_FF:].reshape(d, nch, FFN_CHUNK).transpose(1, 0, 2).astype(BF16)
    wd = w_down.reshape(nch, FFN_CHUNK, d).astype(BF16)
    cw = jnp.concatenate([conv_w[:, :D_FF], conv_b[None, :D_FF], conv_w[:, D_FF:], conv_b[None, D_FF:]], axis=0)
    cw = cw.reshape(2 * (CONV_WIDTH + 1), nch, FFN_CHUNK).transpose(1, 0, 2)
    final = final_gain is not None
    halo_blocks = tile // FFN_HALO
    in_specs = [pl.BlockSpec((tile, d), lambda i: (i, 0)),
                pl.BlockSpec((FFN_HALO, d), lambda i: (jnp.maximum(i * halo_blocks - 1, 0), 0)),
                pl.BlockSpec((1, d), lambda i: (0, 0)),
                pl.BlockSpec((nch, d, FFN_CHUNK), lambda i: (0, 0, 0)),
                pl.BlockSpec((nch, d, FFN_CHUNK), lambda i: (0, 0, 0)),
                pl.BlockSpec((nch, 2 * (CONV_WIDTH + 1), FFN_CHUNK), lambda i: (0, 0, 0)),
                pl.BlockSpec((nch, FFN_CHUNK, d), lambda i: (0, 0, 0))]
    args = [x, x, gain.reshape(1, d), wg, wv, cw, wd]
    if final:
        in_specs.append(pl.BlockSpec((1, d), lambda i: (0, 0)))
        args.append(final_gain.reshape(1, d))
    return pl.pallas_call(
        functools.partial(_ffn_kernel, per_seq=per_seq, final=final),
        grid=(rows // tile,),
        in_specs=in_specs,
        out_specs=pl.BlockSpec((tile, d), lambda i: (i, 0)),
        out_shape=jax.ShapeDtypeStruct((rows, d), F32),
        scratch_shapes=[pltpu.VMEM((tile, d), F32)],
        compiler_params=_params(("parallel",)),
        name="conv_ffn_final" if final else "conv_ffn",
    )(*args)


def _rope_tables(seq):
    pos = jnp.arange(seq, dtype=F32)
    inv = ROPE_THETA ** (-jnp.arange(0, HEAD_DIM, 2, dtype=F32) / HEAD_DIM)
    ang = pos[:, None] * inv[None, :]
    cos, sin = jnp.cos(ang), jnp.sin(ang)
    reps = LANES // HEAD_DIM
    cos_t = jnp.tile(jnp.concatenate([cos, cos], axis=1), (1, reps))
    sin_t = jnp.tile(jnp.concatenate([-sin, sin], axis=1), (1, reps))
    return cos_t, sin_t


def kernel(x, mem, ln_mix, w_in, w_out, mem_norm, w_mem_kv, ln_ffn, w_up, conv_w, conv_b, w_down,
           s5_lambda_re, s5_lambda_im, s5_log_step, s5_b_re, s5_b_im, s5_c_re, s5_c_im, s5_d,
           s5_w_glu, kv_norm, w_kv, final_norm):
    batch, seq, d = x.shape
    depth = ln_mix.shape[0]
    n_s5 = s5_lambda_re.shape[0]
    n_mem = mem.shape[1]
    xf = x.reshape(batch * seq, d)
    memf = mem.reshape(batch * n_mem, d)
    cos_t, sin_t = _rope_tables(seq)
    shared = None
    for l in range(depth):
        mem_kv = norm_matmul(memf, mem_norm[l], w_mem_kv[l].astype(BF16), n_mem, BF16)
        if l < n_s5:
            u, qm = inproj_s5(xf, ln_mix[l], w_in[l].astype(BF16))
            weights = _s5_weights(s5_lambda_re[l], s5_lambda_im[l], s5_log_step[l], s5_b_re[l], s5_b_im[l],
                                  s5_c_re[l], s5_c_im[l], s5_d[l])
            self_pre = s5_ssm(u, weights, batch, seq)
            w_glu = s5_w_glu[l].astype(BF16)
        else:
            q, qm, k3, vt3, kmean = inproj_moba(xf, ln_mix[l], kv_norm, w_in[l].astype(BF16),
                                                w_kv.astype(BF16), cos_t, sin_t, seq)
            if shared is None:
                km = kmean.reshape(batch, seq // MOBA_BLOCK, SELF_WIDTH)
                kmh = km.astype(BF16)
                kml = (km - kmh.astype(F32)).astype(BF16)
                shared = (k3, vt3, kmh, kml)
            self_pre = moba_attention(q, *shared, batch, seq)
            w_glu = None
        xf = mix_out(xf, self_pre, qm, mem_kv, w_glu, w_out[l].astype(BF16), seq)
        xf = conv_ffn(xf, ln_ffn[l], w_up[l], conv_w[l], conv_b[l], w_down[l], seq,
                      final_gain=final_norm if l == depth - 1 else None)
    return xf.reshape(batch, seq, d)
```

```python
import functools
import math

import jax
import jax.numpy as jnp
from jax import lax
from jax.experimental import pallas as pl
from jax.experimental.pallas import tpu as pltpu

F32 = jnp.float32
BF16 = jnp.bfloat16

D_MODEL = 1024
HEAD_DIM = 64
MEM_HEADS = 4
MEM_WIDTH = MEM_HEADS * HEAD_DIM
SELF_WIDTH = D_MODEL - MEM_WIDTH
S5_GROUP = 16
S5_GROUPS = SELF_WIDTH // S5_GROUP
S5_STATE = 64
MOBA_BLOCK = 256
MOBA_TOPK = 3
D_FF = 2816
CONV_WIDTH = 3
ROPE_THETA = 10000.0
NORM_EPS = 1e-6
NEG_INF = -1e30
ATTN_SCALE = HEAD_DIM ** -0.5

LOG2E = math.log2(math.e)
MOBA_GROUP = 4
MOBA_ONES_ROWS = 16

LANES = 128
HEAD_PAIRS = SELF_WIDTH // LANES
S5_CHUNK = 8
S5_LANE_BLOCKS = SELF_WIDTH // LANES
S5_GROUPS_PER_BLOCK = LANES // S5_GROUP
S5_BLOCK_STATE = S5_GROUPS_PER_BLOCK * S5_STATE
S5_ROWS = 256
FFN_CHUNK = 256
FFN_HALO = 16
VMEM_LIMIT = 56 * 1024 * 1024


def _params(sem):
    return pltpu.CompilerParams(dimension_semantics=sem, vmem_limit_bytes=VMEM_LIMIT)


def _rms(x, gain):
    ms = jnp.mean(x * x, axis=-1, keepdims=True)
    return x * lax.rsqrt(ms + NORM_EPS) * gain


def _dot(a, b):
    return jnp.dot(a, b, preferred_element_type=F32)


def _dot_nt(a, b):
    return lax.dot_general(a, b, (((1,), (1,)), ((), ())), preferred_element_type=F32)


def _norm_matmul_kernel(x_ref, g_ref, w_ref, o_ref):
    h = _rms(x_ref[...], g_ref[...]).astype(BF16)
    o_ref[...] = _dot(h, w_ref[...]).astype(o_ref.dtype)


def norm_matmul(x, gain, w, tile, out_dtype):
    rows, d = x.shape
    n = w.shape[1]
    return pl.pallas_call(
        _norm_matmul_kernel,
        grid=(rows // tile,),
        in_specs=[pl.BlockSpec((tile, d), lambda i: (i, 0)),
                  pl.BlockSpec((1, d), lambda i: (0, 0)),
                  pl.BlockSpec((d, n), lambda i: (0, 0))],
        out_specs=pl.BlockSpec((tile, n), lambda i: (i, 0)),
        out_shape=jax.ShapeDtypeStruct((rows, n), out_dtype),
        compiler_params=_params(("parallel",)),
        name="norm_matmul",
    )(x, gain.reshape(1, d), w)


def _inproj_s5_kernel(x_ref, g_ref, w_ref, u_ref, qm_ref):
    h = _rms(x_ref[...], g_ref[...]).astype(BF16)
    z = _dot(h, w_ref[...])
    u_ref[...] = z[:, :SELF_WIDTH]
    qm_ref[...] = z[:, SELF_WIDTH:].astype(BF16)


def inproj_s5(x, gain, w, tile=512):
    rows, d = x.shape
    return pl.pallas_call(
        _inproj_s5_kernel,
        grid=(rows // tile,),
        in_specs=[pl.BlockSpec((tile, d), lambda i: (i, 0)),
                  pl.BlockSpec((1, d), lambda i: (0, 0)),
                  pl.BlockSpec((d, d), lambda i: (0, 0))],
        out_specs=[pl.BlockSpec((tile, SELF_WIDTH), lambda i: (i, 0)),
                   pl.BlockSpec((tile, MEM_WIDTH), lambda i: (i, 0))],
        out_shape=[jax.ShapeDtypeStruct((rows, SELF_WIDTH), F32),
                   jax.ShapeDtypeStruct((rows, MEM_WIDTH), BF16)],
        compiler_params=_params(("parallel",)),
        name="inproj_s5",
    )(x, gain.reshape(1, d), w)


def _rope(z, cos, sin_signed):
    lane = lax.broadcasted_iota(jnp.int32, (z.shape[0], LANES), 1)
    first_half = (lane % HEAD_DIM) < (HEAD_DIM // 2)
    outs = []
    for j in range(z.shape[1] // LANES):
        zj = z[:, j * LANES:(j + 1) * LANES]
        partner = jnp.where(first_half,
                            pltpu.roll(zj, LANES - HEAD_DIM // 2, axis=1),
                            pltpu.roll(zj, HEAD_DIM // 2, axis=1))
        outs.append(zj * cos + partner * sin_signed)
    return jnp.concatenate(outs, axis=1)


def _inproj_moba_kernel(x_ref, gq_ref, gkv_ref, wq_ref, wkv_ref, cos_ref, sin_ref,
                        q_ref, qm_ref, k_ref, vt_ref, km_ref, *, per_seq):
    x = x_ref[...]
    xn = x * lax.rsqrt(jnp.mean(x * x, axis=-1, keepdims=True) + NORM_EPS)
    cos, sin = cos_ref[...], sin_ref[...]
    z = _dot((xn * gq_ref[...]).astype(BF16), wq_ref[...])
    q_ref[...] = (_rope(z[:, :SELF_WIDTH], cos, sin) * (ATTN_SCALE * LOG2E)).astype(BF16)
    qm_ref[...] = z[:, SELF_WIDTH:].astype(BF16)
    kv = _dot((xn * gkv_ref[...]).astype(BF16), wkv_ref[...])
    k = _rope(kv[:, :SELF_WIDTH], cos, sin)
    km_ref[0] = jnp.mean(k, axis=0, keepdims=True)
    vt_ref[0] = kv[:, SELF_WIDTH:].T.astype(BF16)
    lane = lax.broadcasted_iota(jnp.int32, (x.shape[0], LANES), 1)
    lo = lane < HEAD_DIM
    onehot = jnp.where(lane - HEAD_DIM == pl.program_id(0) % per_seq, 1.0, 0.0)
    for pair in range(HEAD_PAIRS):
        kp = k[:, pair * LANES:(pair + 1) * LANES]
        k_ref[pair] = jnp.concatenate([jnp.where(lo, kp, onehot),
                                       jnp.where(lo, pltpu.roll(kp, HEAD_DIM, axis=1), onehot)], axis=1).astype(BF16)


def inproj_moba(x, gq, gkv, wq, wkv, cos, sin, seq):
    rows, d = x.shape
    tile = MOBA_BLOCK
    nblk = rows // tile
    per_seq = seq // tile
    assert per_seq <= LANES - HEAD_DIM, "block one-hot must fit beside the key in one lane tile"
    return pl.pallas_call(
        functools.partial(_inproj_moba_kernel, per_seq=per_seq),
        grid=(nblk,),
        in_specs=[pl.BlockSpec((tile, d), lambda i: (i, 0)),
                  pl.BlockSpec((1, d), lambda i: (0, 0)),
                  pl.BlockSpec((1, d), lambda i: (0, 0)),
                  pl.BlockSpec((d, d), lambda i: (0, 0)),
                  pl.BlockSpec((d, 2 * SELF_WIDTH), lambda i: (0, 0)),
                  pl.BlockSpec((tile, LANES), lambda i: (i % per_seq, 0)),
                  pl.BlockSpec((tile, LANES), lambda i: (i % per_seq, 0))],
        out_specs=[pl.BlockSpec((tile, SELF_WIDTH), lambda i: (i, 0)),
                   pl.BlockSpec((tile, MEM_WIDTH), lambda i: (i, 0)),
                   pl.BlockSpec((HEAD_PAIRS, tile, 2 * LANES), lambda i: (i // per_seq, i % per_seq, 0)),
                   pl.BlockSpec((1, SELF_WIDTH, tile), lambda i: (i, 0, 0)),
                   pl.BlockSpec((1, 1, SELF_WIDTH), lambda i: (i, 0, 0))],
        out_shape=[jax.ShapeDtypeStruct((rows, SELF_WIDTH), BF16),
                   jax.ShapeDtypeStruct((rows, MEM_WIDTH), BF16),
                   jax.ShapeDtypeStruct((rows // seq * HEAD_PAIRS, seq, 2 * LANES), BF16),
                   jax.ShapeDtypeStruct((nblk, SELF_WIDTH, tile), BF16),
                   jax.ShapeDtypeStruct((nblk, 1, SELF_WIDTH), F32)],
        compiler_params=_params(("parallel",)),
        name="inproj_moba",
    )(x, gq.reshape(1, d), gkv.reshape(1, d), wq, wkv, cos, sin)


def _s5_weights(lam_re, lam_im, log_step, b_re, b_im, c_re, c_im, d_skip):
    L = S5_CHUNK
    G, N, P = S5_GROUPS, S5_STATE, S5_GROUP
    JB, GL = S5_LANE_BLOCKS, S5_GROUPS_PER_BLOCK
    dt = jnp.exp(log_step)[:, None]
    mag = jnp.exp(lam_re * dt)
    ab_re, ab_im = mag * jnp.cos(lam_im * dt), mag * jnp.sin(lam_im * dt)
    den = lam_re * lam_re + lam_im * lam_im
    num_re, num_im = ab_re - 1.0, ab_im
    coef_re = (num_re * lam_re + num_im * lam_im) / den
    coef_im = (num_im * lam_re - num_re * lam_im) / den
    bb_re = coef_re[..., None] * b_re - coef_im[..., None] * b_im
    bb_im = coef_re[..., None] * b_im + coef_im[..., None] * b_re

    def power(j):
        j = jnp.asarray(j, F32).reshape((-1, 1, 1))
        m = jnp.exp(j * (lam_re * dt))
        return m * jnp.cos(j * (lam_im * dt)), m * jnp.sin(j * (lam_im * dt))

    pw_re, pw_im = power(jnp.arange(L + 1))
    cb_re = (jnp.einsum('gpn,dgn,gnq->dgpq', c_re, pw_re, bb_re)
             - jnp.einsum('gpn,dgn,gnq->dgpq', c_re, pw_im, bb_im)
             - jnp.einsum('gpn,dgn,gnq->dgpq', c_im, pw_re, bb_im)
             - jnp.einsum('gpn,dgn,gnq->dgpq', c_im, pw_im, bb_re))
    cb_re = cb_re.at[0].add(jnp.einsum('gp,pq->gpq', d_skip, jnp.eye(P, dtype=F32)))
    s_ix = jnp.arange(L)[:, None]
    t_ix = jnp.arange(L)[None, :]
    delta = jnp.clip(t_ix - s_ix, 0, L - 1)
    kst = jnp.where((t_ix >= s_ix)[..., None, None, None], cb_re[delta], 0.0)
    kst = kst.reshape(L, L, JB, GL, P, P)
    eye = jnp.eye(GL, dtype=F32)
    intra = jnp.einsum('stjgpq,gh->jsgqthp', kst, eye).reshape(JB, L * LANES, L * LANES)

    rev_re, rev_im = pw_re[L - 1 - jnp.arange(L)], pw_im[L - 1 - jnp.arange(L)]
    inj_re = rev_re[..., None] * bb_re[None] - rev_im[..., None] * bb_im[None]
    inj_im = rev_re[..., None] * bb_im[None] + rev_im[..., None] * bb_re[None]
    inj = jnp.stack([inj_re, inj_im], axis=0).reshape(2, L, JB, GL, N, P)
    inject = jnp.einsum('csjgnp,gh->jsgpchn', inj, eye).reshape(JB, L * LANES, 2 * S5_BLOCK_STATE)

    nx_re, nx_im = pw_re[1:], pw_im[1:]
    ro_re = c_re[None] * nx_re[:, :, None, :] - c_im[None] * nx_im[:, :, None, :]
    ro_im = -(c_re[None] * nx_im[:, :, None, :] + c_im[None] * nx_re[:, :, None, :])
    ro = jnp.stack([ro_re, ro_im], axis=0).reshape(2, L, JB, GL, P, N)
    readout = jnp.einsum('ctjgpn,gh->jcgnthp', ro, eye).reshape(JB, 2 * S5_BLOCK_STATE, L * LANES)

    levels = int(math.log2(S5_ROWS))
    sc_re, sc_im = power(L * (2 ** jnp.arange(levels)))
    scan = jnp.stack([sc_re, sc_im], axis=1).reshape(levels, 2, JB, GL * N)
    scan = scan.transpose(2, 0, 1, 3).reshape(JB, levels, 2 * S5_BLOCK_STATE)
    return intra.astype(BF16), inject.astype(BF16), readout.astype(BF16), scan


def _s5_kernel(u_ref, intra_ref, inject_ref, readout_ref, scan_ref, y_ref, h_ref):
    rows = u_ref.shape[0]
    ns = S5_BLOCK_STATE

    @pl.when(pl.program_id(2) == 0)
    def _():
        h_ref[...] = jnp.zeros_like(h_ref)

    x = jnp.concatenate([u_ref[:, t, :] for t in range(S5_CHUNK)], axis=1).astype(BF16)
    e = _dot(x, inject_ref[0])
    er, ei = e[:, :ns], e[:, ns:]
    h_in = h_ref[...]
    hr, hi = h_in[:, :ns], h_in[:, ns:]
    row = lax.broadcasted_iota(jnp.int32, (rows, ns), 0)
    first = row == 0
    a = scan_ref[0]
    ar, ai = a[0:1, :ns], a[0:1, ns:]
    er = er + jnp.where(first, ar * hr - ai * hi, 0.0)
    ei = ei + jnp.where(first, ar * hi + ai * hr, 0.0)
    for k in range(a.shape[0]):
        d = 1 << k
        ar, ai = a[k:k + 1, :ns], a[k:k + 1, ns:]
        keep = row >= d
        sr = jnp.where(keep, pltpu.roll(er, d, axis=0), 0.0)
        si = jnp.where(keep, pltpu.roll(ei, d, axis=0), 0.0)
        er, ei = er + ar * sr - ai * si, ei + ar * si + ai * sr
    h_ref[...] = jnp.concatenate([er[rows - 1:rows], ei[rows - 1:rows]], axis=1)
    hx_r = jnp.where(first, hr, pltpu.roll(er, 1, axis=0))
    hx_i = jnp.where(first, hi, pltpu.roll(ei, 1, axis=0))
    hx = jnp.concatenate([hx_r, hx_i], axis=1).astype(BF16)
    y = _dot(x, intra_ref[0]) + _dot(hx, readout_ref[0])
    for t in range(S5_CHUNK):
        y_ref[:, t, :] = y[:, t * LANES:(t + 1) * LANES]


def s5_ssm(u, weights, batch, seq):
    intra, inject, readout, scan = weights
    rows_per_seq = seq // S5_CHUNK
    tile = min(S5_ROWS, rows_per_seq)
    assert tile == S5_ROWS, "sequence too short for the S5 row tile"
    nrt = rows_per_seq // tile
    u3 = u.reshape(batch * rows_per_seq, S5_CHUNK, SELF_WIDTH)
    wide = S5_CHUNK * LANES
    y3 = pl.pallas_call(
        _s5_kernel,
        grid=(batch, S5_LANE_BLOCKS, nrt),
        in_specs=[pl.BlockSpec((tile, S5_CHUNK, LANES), lambda b, j, r: (b * nrt + r, 0, j)),
                  pl.BlockSpec((1, wide, wide), lambda b, j, r: (j, 0, 0)),
                  pl.BlockSpec((1, wide, 2 * S5_BLOCK_STATE), lambda b, j, r: (j, 0, 0)),
                  pl.BlockSpec((1, 2 * S5_BLOCK_STATE, wide), lambda b, j, r: (j, 0, 0)),
                  pl.BlockSpec((1, scan.shape[1], 2 * S5_BLOCK_STATE), lambda b, j, r: (j, 0, 0))],
        out_specs=pl.BlockSpec((tile, S5_CHUNK, LANES), lambda b, j, r: (b * nrt + r, 0, j)),
        out_shape=jax.ShapeDtypeStruct(u3.shape, F32),
        scratch_shapes=[pltpu.VMEM((1, 2 * S5_BLOCK_STATE), F32)],
        compiler_params=_params(("parallel", "parallel", "arbitrary")),
        name="s5_ssm",
    )(u3, intra, inject, readout, scan)
    return y3.reshape(u.shape)


def _moba_kernel(q_ref, k_ref, vt_ref, kmh_ref, kml_ref, o_ref, qa_ref, qo_ref, m_ref, acc_ref, sa_ref, sb_ref):
    j = pl.program_id(2)
    tq = q_ref.shape[0]
    blk_rows = MOBA_BLOCK
    q = q_ref[...].astype(F32)
    lane = lax.broadcasted_iota(jnp.int32, q.shape, 1)
    lo = lane < HEAD_DIM
    blk = jnp.where(lo, 1e9, (lane - HEAD_DIM).astype(F32))
    past = blk < j.astype(F32)
    ones = jnp.ones((MOBA_ONES_ROWS, blk_rows), BF16)
    kpos = lax.broadcasted_iota(jnp.int32, (blk_rows, 2 * tq), 0)
    qpos = lax.broadcasted_iota(jnp.int32, (blk_rows, 2 * tq), 1) % tq

    zeros = jnp.zeros(q.shape, F32)
    for e in range(2):
        keep = lo if e == 0 else jnp.logical_not(lo)
        qe = jnp.where(keep, q, 0.0)
        qe16 = qe.astype(BF16)
        gate = _dot_nt(qe16, kmh_ref[0, 0]) + _dot_nt(qe16, kml_ref[0, 0])
        g = jnp.where(past, gate, -jnp.inf)
        bias = jnp.full(gate.shape, NEG_INF, F32)
        for _ in range(MOBA_TOPK):
            top = jnp.max(g, axis=1, keepdims=True)
            idx = jnp.min(jnp.where(g == top, blk, 1e9), axis=1, keepdims=True)
            hit = blk == idx
            bias = jnp.where(hit, jnp.where(past, 0.0, NEG_INF), bias)
            g = jnp.where(hit, -jnp.inf, g)
        q_head = qe if e == 0 else pltpu.roll(qe, HEAD_DIM, axis=1)
        halves = [zeros, zeros]
        halves[e] = jnp.where(lo, q_head, bias)
        qa_ref[e * tq:(e + 1) * tq, :] = jnp.concatenate(halves, axis=1).astype(BF16)
        halves[e] = q_head
        qo_ref[e * tq:(e + 1) * tq, :] = jnp.concatenate(halves, axis=1).astype(BF16)

    def value_rows(n):
        v = vt_ref[n]
        return jnp.concatenate([v[:HEAD_DIM], ones, v[HEAD_DIM:], ones], axis=0)

    s = _dot_nt(k_ref[0, pl.ds(pl.multiple_of(j * blk_rows, blk_rows), blk_rows), :], qo_ref[...])
    s = jnp.where(kpos <= qpos, s, NEG_INF)
    m0 = jnp.max(s, axis=0, keepdims=True)
    m_ref[...] = m0
    acc_ref[...] = _dot(value_rows(j), jnp.exp2(s - m0).astype(BF16))

    grp_rows = MOBA_GROUP * blk_rows
    ngroups = (j + MOBA_GROUP - 1) // MOBA_GROUP
    last_group = k_ref.shape[1] // grp_rows - 1

    def scores(grp, dst_ref):
        kc = k_ref[0, pl.ds(pl.multiple_of(grp * grp_rows, grp_rows), grp_rows), :]
        dst_ref[...] = _dot_nt(kc, qa_ref[...])

    def absorb(grp, src_ref):
        s = src_ref[...]
        m_old = m_ref[...]
        m_new = jnp.maximum(m_old, jnp.max(s, axis=0, keepdims=True))
        p = jnp.exp2(s - m_new).astype(BF16)
        pv = None
        for i in range(MOBA_GROUP):
            part = _dot(value_rows(grp * MOBA_GROUP + i), p[i * blk_rows:(i + 1) * blk_rows])
            pv = part if pv is None else pv + part
        acc_ref[...] = jnp.exp2(m_old - m_new) * acc_ref[...] + pv
        m_ref[...] = m_new

    scores(0, sa_ref)

    def body(t, carry):
        scores(2 * t + 1, sb_ref)
        absorb(2 * t, sa_ref)
        scores(jnp.minimum(2 * t + 2, last_group), sa_ref)
        absorb(2 * t + 1, sb_ref)
        return carry

    lax.fori_loop(0, ngroups // 2, body, 0)

    @pl.when(ngroups % 2 == 1)
    def _():
        absorb(ngroups - 1, sa_ref)

    acc = acc_ref[...]
    rows = HEAD_DIM + MOBA_ONES_ROWS
    outs = [acc[e * rows:e * rows + HEAD_DIM, e * tq:(e + 1) * tq]
            / acc[e * rows + HEAD_DIM:e * rows + HEAD_DIM + 1, e * tq:(e + 1) * tq] for e in range(2)]
    o_ref[...] = jnp.concatenate(outs, axis=0).T.astype(o_ref.dtype)


def moba_attention(q, kaug, vt3, kmh, kml, batch, seq):
    rows = q.shape[0]
    nb = seq // MOBA_BLOCK
    tq = MOBA_BLOCK
    assert nb % MOBA_GROUP == 0
    acc_rows = 2 * (HEAD_DIM + MOBA_ONES_ROWS)
    return pl.pallas_call(
        _moba_kernel,
        grid=(batch, HEAD_PAIRS, nb),
        in_specs=[pl.BlockSpec((tq, LANES), lambda b, h, j: (b * nb + j, h)),
                  pl.BlockSpec((1, seq, 2 * LANES), lambda b, h, j: (b * HEAD_PAIRS + h, 0, 0)),
                  pl.BlockSpec((nb, LANES, MOBA_BLOCK), lambda b, h, j: (b, h, 0)),
                  pl.BlockSpec((1, 1, LANES, LANES), lambda b, h, j: (b, h, 0, 0)),
                  pl.BlockSpec((1, 1, LANES, LANES), lambda b, h, j: (b, h, 0, 0))],
        out_specs=pl.BlockSpec((tq, LANES), lambda b, h, j: (b * nb + j, h)),
        out_shape=jax.ShapeDtypeStruct((rows, SELF_WIDTH), BF16),
        scratch_shapes=[pltpu.VMEM((2 * tq, 2 * LANES), BF16),
                        pltpu.VMEM((2 * tq, 2 * LANES), BF16),
                        pltpu.VMEM((1, 2 * tq), F32),
                        pltpu.VMEM((acc_rows, 2 * tq), F32),
                        pltpu.VMEM((MOBA_GROUP * MOBA_BLOCK, 2 * tq), F32),
                        pltpu.VMEM((MOBA_GROUP * MOBA_BLOCK, 2 * tq), F32)],
        compiler_params=_params(("parallel", "parallel", "arbitrary")),
        name="moba_attention",
    )(q, kaug, vt3, kmh, kml)


def _key_mean_operand(kmean, batch, seq):
    nb = seq // MOBA_BLOCK
    km = kmean.reshape(batch, nb, HEAD_PAIRS, LANES).transpose(0, 2, 1, 3)
    km = jnp.pad(km, ((0, 0), (0, 0), (HEAD_DIM, LANES - HEAD_DIM - nb), (0, 0)))
    hi = km.astype(BF16)
    lo = (km - hi.astype(F32)).astype(BF16)
    return hi, lo


def _memory_heads(qm, km, vm):
    lane = lax.broadcasted_iota(jnp.int32, (qm.shape[0], LANES), 1)
    lo = lane < HEAD_DIM
    outs = []
    for pair in range(MEM_WIDTH // LANES):
        sl = slice(pair * LANES, (pair + 1) * LANES)
        qp, kp, vp = qm[:, sl], km[:, sl], vm[:, sl]
        zero = jnp.zeros_like(qp)
        res = []
        for keep in (lo, jnp.logical_not(lo)):
            s = _dot_nt(jnp.where(keep, qp, zero), kp) * ATTN_SCALE
            p = jnp.exp(s - jnp.max(s, axis=-1, keepdims=True))
            p = p / jnp.sum(p, axis=-1, keepdims=True)
            res.append(_dot(p.astype(BF16), vp))
        outs.append(jnp.where(lo, res[0], res[1]))
    return jnp.concatenate(outs, axis=1)


def _mixout_kernel(x_ref, s_ref, qm_ref, km_ref, vm_ref, *rest, glu):
    if glu:
        wg_ref, wo_ref, o_ref = rest
        y = jax.nn.gelu(s_ref[...])
        gate = _dot(y.astype(BF16), wg_ref[...])
        self_out = (y / (1.0 + jnp.exp(-gate))).astype(BF16)
    else:
        wo_ref, o_ref = rest
        self_out = s_ref[...]
    mem_out = _memory_heads(qm_ref[...], km_ref[...], vm_ref[...]).astype(BF16)
    mixed = jnp.concatenate([self_out, mem_out], axis=1)
    o_ref[...] = x_ref[...] + _dot(mixed, wo_ref[...])


def mix_out(x, self_pre, qm, mem_kv, w_glu, w_out, seq, tile=512):
    rows, d = x.shape
    n_mem = mem_kv.shape[0] // (rows // seq)
    per_seq = seq // tile
    glu = w_glu is not None
    in_specs = [pl.BlockSpec((tile, d), lambda i: (i, 0)),
                pl.BlockSpec((tile, SELF_WIDTH), lambda i: (i, 0)),
                pl.BlockSpec((tile, MEM_WIDTH), lambda i: (i, 0)),
                pl.BlockSpec((n_mem, MEM_WIDTH), lambda i: (i // per_seq, 0)),
                pl.BlockSpec((n_mem, MEM_WIDTH), lambda i: (i // per_seq, 1))]
    args = [x, self_pre, qm, mem_kv, mem_kv]
    if glu:
        in_specs.append(pl.BlockSpec((SELF_WIDTH, SELF_WIDTH), lambda i: (0, 0)))
        args.append(w_glu)
    in_specs.append(pl.BlockSpec((d, d), lambda i: (0, 0)))
    args.append(w_out)
    return pl.pallas_call(
        functools.partial(_mixout_kernel, glu=glu),
        grid=(rows // tile,),
        in_specs=in_specs,
        out_specs=pl.BlockSpec((tile, d), lambda i: (i, 0)),
        out_shape=jax.ShapeDtypeStruct((rows, d), F32),
        compiler_params=_params(("parallel",)),
        name="mix_out_glu" if glu else "mix_out",
    )(*args)


def _ffn_kernel(x_ref, halo_ref, g_ref, wg_ref, wv_ref, cw_ref, wd_ref, *rest, per_seq, final):
    if final:
        gf_ref, o_ref, acc_ref = rest
    else:
        o_ref, acc_ref = rest
    tile = x_ref.shape[0]
    x = x_ref[...]
    gain = g_ref[...]
    halo = jnp.where(pl.program_id(0) % per_seq == 0, 0.0, halo_ref[...])
    hcat = jnp.concatenate([_rms(halo, gain).astype(BF16), _rms(x, gain).astype(BF16)], axis=0)
    acc_ref[...] = x

    def conv(u, w):
        return (w[0:1] * pltpu.roll(u, 2, axis=0)[FFN_HALO:] + w[1:2] * pltpu.roll(u, 1, axis=0)[FFN_HALO:]
                + w[2:3] * u[FFN_HALO:] + w[3:4])

    def chunk(c, carry):
        cw = cw_ref[c]
        gg = conv(_dot(hcat, wg_ref[c]), cw[0:4])
        vv = conv(_dot(hcat, wv_ref[c]), cw[4:8])
        act = (gg / (1.0 + jnp.exp(-gg)) * vv).astype(BF16)
        acc_ref[...] += _dot(act, wd_ref[c])
        return carry

    lax.fori_loop(0, wg_ref.shape[0], chunk, 0)
    out = acc_ref[...]
    if final:
        out = _rms(out, gf_ref[...])
    o_ref[...] = out


def conv_ffn(x, gain, w_up, conv_w, conv_b, w_down, seq, final_gain=None, tile=512):
    rows, d = x.shape
    nch = D_FF // FFN_CHUNK
    per_seq = seq // tile
    wg = w_up[:, :D_FF].reshape(d, nch, FFN_CHUNK).transpose(1, 0, 2).astype(BF16)
    wv = w_up[:, D_FF:].reshape(d, nch, FFN_CHUNK).transpose(1, 0, 2).astype(BF16)
    wd = w_down.reshape(nch, FFN_CHUNK, d).astype(BF16)
    cw = jnp.concatenate([conv_w[:, :D_FF], conv_b[None, :D_FF], conv_w[:, D_FF:], conv_b[None, D_FF:]], axis=0)
    cw = cw.reshape(2 * (CONV_WIDTH + 1), nch, FFN_CHUNK).transpose(1, 0, 2)
    final = final_gain is not None
    halo_blocks = tile // FFN_HALO
    in_specs = [pl.BlockSpec((tile, d), lambda i: (i, 0)),
                pl.BlockSpec((FFN_HALO, d), lambda i: (jnp.maximum(i * halo_blocks - 1, 0), 0)),
                pl.BlockSpec((1, d), lambda i: (0, 0)),
                pl.BlockSpec((nch, d, FFN_CHUNK), lambda i: (0, 0, 0)),
                pl.BlockSpec((nch, d, FFN_CHUNK), lambda i: (0, 0, 0)),
                pl.BlockSpec((nch, 2 * (CONV_WIDTH + 1), FFN_CHUNK), lambda i: (0, 0, 0)),
                pl.BlockSpec((nch, FFN_CHUNK, d), lambda i: (0, 0, 0))]
    args = [x, x, gain.reshape(1, d), wg, wv, cw, wd]
    if final:
        in_specs.append(pl.BlockSpec((1, d), lambda i: (0, 0)))
        args.append(final_gain.reshape(1, d))
    return pl.pallas_call(
        functools.partial(_ffn_kernel, per_seq=per_seq, final=final),
        grid=(rows // tile,),
        in_specs=in_specs,
        out_specs=pl.BlockSpec((tile, d), lambda i: (i, 0)),
        out_shape=jax.ShapeDtypeStruct((rows, d), F32),
        scratch_shapes=[pltpu.VMEM((tile, d), F32)],
        compiler_params=_params(("parallel",)),
        name="conv_ffn_final" if final else "conv_ffn",
    )(*args)


def _rope_tables(seq):
    pos = jnp.arange(seq, dtype=F32)
    inv = ROPE_THETA ** (-jnp.arange(0, HEAD_DIM, 2, dtype=F32) / HEAD_DIM)
    ang = pos[:, None] * inv[None, :]
    cos, sin = jnp.cos(ang), jnp.sin(ang)
    reps = LANES // HEAD_DIM
    cos_t = jnp.tile(jnp.concatenate([cos, cos], axis=1), (1, reps))
    sin_t = jnp.tile(jnp.concatenate([-sin, sin], axis=1), (1, reps))
    return cos_t, sin_t


def kernel(x, mem, ln_mix, w_in, w_out, mem_norm, w_mem_kv, ln_ffn, w_up, conv_w, conv_b, w_down,
           s5_lambda_re, s5_lambda_im, s5_log_step, s5_b_re, s5_b_im, s5_c_re, s5_c_im, s5_d,
           s5_w_glu, kv_norm, w_kv, final_norm):
    batch, seq, d = x.shape
    depth = ln_mix.shape[0]
    n_s5 = s5_lambda_re.shape[0]
    n_mem = mem.shape[1]
    xf = x.reshape(batch * seq, d)
    memf = mem.reshape(batch * n_mem, d)
    cos_t, sin_t = _rope_tables(seq)
    shared = None
    for l in range(depth):
        mem_kv = norm_matmul(memf, mem_norm[l], w_mem_kv[l].astype(BF16), n_mem, BF16)
        if l < n_s5:
            u, qm = inproj_s5(xf, ln_mix[l], w_in[l].astype(BF16))
            weights = _s5_weights(s5_lambda_re[l], s5_lambda_im[l], s5_log_step[l], s5_b_re[l], s5_b_im[l],
                                  s5_c_re[l], s5_c_im[l], s5_d[l])
            self_pre = s5_ssm(u, weights, batch, seq)
            w_glu = s5_w_glu[l].astype(BF16)
        else:
            q, qm, k3, vt3, kmean = inproj_moba(xf, ln_mix[l], kv_norm, w_in[l].astype(BF16),
                                                w_kv.astype(BF16), cos_t, sin_t, seq)
            if shared is None:
                shared = (k3, vt3, *_key_mean_operand(kmean, batch, seq))
            self_pre = moba_attention(q, *shared, batch, seq)
            w_glu = None
        xf = mix_out(xf, self_pre, qm, mem_kv, w_glu, w_out[l].astype(BF16), seq)
        xf = conv_ffn(xf, ln_ffn[l], w_up[l], conv_w[l], conv_b[l], w_down[l], seq,
                      final_gain=final_norm if l == depth - 1 else None)
    return xf.reshape(batch, seq, d)
```

```python
import functools
import math

import jax
import jax.numpy as jnp
from jax import lax
from jax.experimental import pallas as pl
from jax.experimental.pallas import tpu as pltpu

F32 = jnp.float32
BF16 = jnp.bfloat16

D_MODEL = 1024
HEAD_DIM = 64
MEM_HEADS = 4
MEM_WIDTH = MEM_HEADS * HEAD_DIM
SELF_WIDTH = D_MODEL - MEM_WIDTH
S5_GROUP = 16
S5_GROUPS = SELF_WIDTH // S5_GROUP
S5_STATE = 64
MOBA_BLOCK = 256
MOBA_TOPK = 3
D_FF = 2816
CONV_WIDTH = 3
ROPE_THETA = 10000.0
NORM_EPS = 1e-6
NEG_INF = -1e30
ATTN_SCALE = HEAD_DIM ** -0.5

LOG2E = math.log2(math.e)
MOBA_GROUP = 4
MOBA_ONES_ROWS = 16

LANES = 128
HEAD_PAIRS = SELF_WIDTH // LANES
S5_CHUNK = 8
S5_LANE_BLOCKS = SELF_WIDTH // LANES
S5_GROUPS_PER_BLOCK = LANES // S5_GROUP
S5_BLOCK_STATE = S5_GROUPS_PER_BLOCK * S5_STATE
S5_ROWS = 256
FFN_CHUNK = 256
FFN_HALO = 16
VMEM_LIMIT = 56 * 1024 * 1024


def _params(sem, flags=None):
    return pltpu.CompilerParams(dimension_semantics=sem, vmem_limit_bytes=VMEM_LIMIT, flags=flags)


def _rms(x, gain):
    ms = jnp.mean(x * x, axis=-1, keepdims=True)
    return x * lax.rsqrt(ms + NORM_EPS) * gain


def _dot(a, b):
    return jnp.dot(a, b, preferred_element_type=F32)


def _dot_nt(a, b):
    return lax.dot_general(a, b, (((1,), (1,)), ((), ())), preferred_element_type=F32)


def _norm_matmul_kernel(x_ref, g_ref, w_ref, o_ref):
    h = _rms(x_ref[...], g_ref[...]).astype(BF16)
    o_ref[...] = _dot(h, w_ref[...]).astype(o_ref.dtype)


def norm_matmul(x, gain, w, tile, out_dtype):
    rows, d = x.shape
    n = w.shape[1]
    return pl.pallas_call(
        _norm_matmul_kernel,
        grid=(rows // tile,),
        in_specs=[pl.BlockSpec((tile, d), lambda i: (i, 0)),
                  pl.BlockSpec((1, d), lambda i: (0, 0)),
                  pl.BlockSpec((d, n), lambda i: (0, 0))],
        out_specs=pl.BlockSpec((tile, n), lambda i: (i, 0)),
        out_shape=jax.ShapeDtypeStruct((rows, n), out_dtype),
        compiler_params=_params(("parallel",)),
        name="norm_matmul",
    )(x, gain.reshape(1, d), w)


def _inproj_s5_kernel(x_ref, g_ref, w_ref, u_ref, qm_ref):
    h = _rms(x_ref[...], g_ref[...]).astype(BF16)
    z = _dot(h, w_ref[...])
    u_ref[...] = z[:, :SELF_WIDTH]
    qm_ref[...] = z[:, SELF_WIDTH:].astype(BF16)


def inproj_s5(x, gain, w, tile=512):
    rows, d = x.shape
    return pl.pallas_call(
        _inproj_s5_kernel,
        grid=(rows // tile,),
        in_specs=[pl.BlockSpec((tile, d), lambda i: (i, 0)),
                  pl.BlockSpec((1, d), lambda i: (0, 0)),
                  pl.BlockSpec((d, d), lambda i: (0, 0))],
        out_specs=[pl.BlockSpec((tile, SELF_WIDTH), lambda i: (i, 0)),
                   pl.BlockSpec((tile, MEM_WIDTH), lambda i: (i, 0))],
        out_shape=[jax.ShapeDtypeStruct((rows, SELF_WIDTH), F32),
                   jax.ShapeDtypeStruct((rows, MEM_WIDTH), BF16)],
        compiler_params=_params(("parallel",)),
        name="inproj_s5",
    )(x, gain.reshape(1, d), w)


def _rope(z, cos, sin_signed):
    lane = lax.broadcasted_iota(jnp.int32, (z.shape[0], LANES), 1)
    first_half = (lane % HEAD_DIM) < (HEAD_DIM // 2)
    outs = []
    for j in range(z.shape[1] // LANES):
        zj = z[:, j * LANES:(j + 1) * LANES]
        partner = jnp.where(first_half,
                            pltpu.roll(zj, LANES - HEAD_DIM // 2, axis=1),
                            pltpu.roll(zj, HEAD_DIM // 2, axis=1))
        outs.append(zj * cos + partner * sin_signed)
    return jnp.concatenate(outs, axis=1)


def _inproj_moba_kernel(x_ref, gq_ref, gkv_ref, wq_ref, wkv_ref, cos_ref, sin_ref,
                        q_ref, qm_ref, k_ref, vt_ref, km_ref, *, per_seq):
    x = x_ref[...]
    xn = x * lax.rsqrt(jnp.mean(x * x, axis=-1, keepdims=True) + NORM_EPS)
    cos, sin = cos_ref[...], sin_ref[...]
    z = _dot((xn * gq_ref[...]).astype(BF16), wq_ref[...])
    q_ref[...] = (_rope(z[:, :SELF_WIDTH], cos, sin) * (ATTN_SCALE * LOG2E)).astype(BF16)
    qm_ref[...] = z[:, SELF_WIDTH:].astype(BF16)
    kv = _dot((xn * gkv_ref[...]).astype(BF16), wkv_ref[...])
    k = _rope(kv[:, :SELF_WIDTH], cos, sin)
    km_ref[0] = jnp.mean(k, axis=0, keepdims=True)
    vt_ref[0] = kv[:, SELF_WIDTH:].T.astype(BF16)
    lane = lax.broadcasted_iota(jnp.int32, (x.shape[0], LANES), 1)
    lo = lane < HEAD_DIM
    onehot = jnp.where(lane - HEAD_DIM == pl.program_id(0) % per_seq, 1.0, 0.0)
    for pair in range(HEAD_PAIRS):
        kp = k[:, pair * LANES:(pair + 1) * LANES]
        k_ref[pair] = jnp.concatenate([jnp.where(lo, kp, onehot),
                                       jnp.where(lo, pltpu.roll(kp, HEAD_DIM, axis=1), onehot)], axis=1).astype(BF16)


def inproj_moba(x, gq, gkv, wq, wkv, cos, sin, seq):
    rows, d = x.shape
    tile = MOBA_BLOCK
    nblk = rows // tile
    per_seq = seq // tile
    assert per_seq <= LANES - HEAD_DIM, "block one-hot must fit beside the key in one lane tile"
    return pl.pallas_call(
        functools.partial(_inproj_moba_kernel, per_seq=per_seq),
        grid=(nblk,),
        in_specs=[pl.BlockSpec((tile, d), lambda i: (i, 0)),
                  pl.BlockSpec((1, d), lambda i: (0, 0)),
                  pl.BlockSpec((1, d), lambda i: (0, 0)),
                  pl.BlockSpec((d, d), lambda i: (0, 0)),
                  pl.BlockSpec((d, 2 * SELF_WIDTH), lambda i: (0, 0)),
                  pl.BlockSpec((tile, LANES), lambda i: (i % per_seq, 0)),
                  pl.BlockSpec((tile, LANES), lambda i: (i % per_seq, 0))],
        out_specs=[pl.BlockSpec((tile, SELF_WIDTH), lambda i: (i, 0)),
                   pl.BlockSpec((tile, MEM_WIDTH), lambda i: (i, 0)),
                   pl.BlockSpec((HEAD_PAIRS, tile, 2 * LANES), lambda i: (i // per_seq, i % per_seq, 0)),
                   pl.BlockSpec((1, SELF_WIDTH, tile), lambda i: (i, 0, 0)),
                   pl.BlockSpec((1, 1, SELF_WIDTH), lambda i: (i, 0, 0))],
        out_shape=[jax.ShapeDtypeStruct((rows, SELF_WIDTH), BF16),
                   jax.ShapeDtypeStruct((rows, MEM_WIDTH), BF16),
                   jax.ShapeDtypeStruct((rows // seq * HEAD_PAIRS, seq, 2 * LANES), BF16),
                   jax.ShapeDtypeStruct((nblk, SELF_WIDTH, tile), BF16),
                   jax.ShapeDtypeStruct((nblk, 1, SELF_WIDTH), F32)],
        compiler_params=_params(("parallel",)),
        name="inproj_moba",
    )(x, gq.reshape(1, d), gkv.reshape(1, d), wq, wkv, cos, sin)


def _s5_weights(lam_re, lam_im, log_step, b_re, b_im, c_re, c_im, d_skip):
    L = S5_CHUNK
    G, N, P = S5_GROUPS, S5_STATE, S5_GROUP
    JB, GL = S5_LANE_BLOCKS, S5_GROUPS_PER_BLOCK
    dt = jnp.exp(log_step)[:, None]
    mag = jnp.exp(lam_re * dt)
    ab_re, ab_im = mag * jnp.cos(lam_im * dt), mag * jnp.sin(lam_im * dt)
    den = lam_re * lam_re + lam_im * lam_im
    num_re, num_im = ab_re - 1.0, ab_im
    coef_re = (num_re * lam_re + num_im * lam_im) / den
    coef_im = (num_im * lam_re - num_re * lam_im) / den
    bb_re = coef_re[..., None] * b_re - coef_im[..., None] * b_im
    bb_im = coef_re[..., None] * b_im + coef_im[..., None] * b_re

    def power(j):
        j = jnp.asarray(j, F32).reshape((-1, 1, 1))
        m = jnp.exp(j * (lam_re * dt))
        return m * jnp.cos(j * (lam_im * dt)), m * jnp.sin(j * (lam_im * dt))

    pw_re, pw_im = power(jnp.arange(L + 1))
    cb_re = (jnp.einsum('gpn,dgn,gnq->dgpq', c_re, pw_re, bb_re)
             - jnp.einsum('gpn,dgn,gnq->dgpq', c_re, pw_im, bb_im)
             - jnp.einsum('gpn,dgn,gnq->dgpq', c_im, pw_re, bb_im)
             - jnp.einsum('gpn,dgn,gnq->dgpq', c_im, pw_im, bb_re))
    cb_re = cb_re.at[0].add(jnp.einsum('gp,pq->gpq', d_skip, jnp.eye(P, dtype=F32)))
    s_ix = jnp.arange(L)[:, None]
    t_ix = jnp.arange(L)[None, :]
    delta = jnp.clip(t_ix - s_ix, 0, L - 1)
    kst = jnp.where((t_ix >= s_ix)[..., None, None, None], cb_re[delta], 0.0)
    kst = kst.reshape(L, L, JB, GL, P, P)
    eye = jnp.eye(GL, dtype=F32)
    intra = jnp.einsum('stjgpq,gh->jsgqthp', kst, eye).reshape(JB, L * LANES, L * LANES)

    rev_re, rev_im = pw_re[L - 1 - jnp.arange(L)], pw_im[L - 1 - jnp.arange(L)]
    inj_re = rev_re[..., None] * bb_re[None] - rev_im[..., None] * bb_im[None]
    inj_im = rev_re[..., None] * bb_im[None] + rev_im[..., None] * bb_re[None]
    inj = jnp.stack([inj_re, inj_im], axis=0).reshape(2, L, JB, GL, N, P)
    inject = jnp.einsum('csjgnp,gh->jsgpchn', inj, eye).reshape(JB, L * LANES, 2 * S5_BLOCK_STATE)

    nx_re, nx_im = pw_re[1:], pw_im[1:]
    ro_re = c_re[None] * nx_re[:, :, None, :] - c_im[None] * nx_im[:, :, None, :]
    ro_im = -(c_re[None] * nx_im[:, :, None, :] + c_im[None] * nx_re[:, :, None, :])
    ro = jnp.stack([ro_re, ro_im], axis=0).reshape(2, L, JB, GL, P, N)
    readout = jnp.einsum('ctjgpn,gh->jcgnthp', ro, eye).reshape(JB, 2 * S5_BLOCK_STATE, L * LANES)

    levels = int(math.log2(S5_ROWS))
    sc_re, sc_im = power(L * (2 ** jnp.arange(levels)))
    scan = jnp.stack([sc_re, sc_im], axis=1).reshape(levels, 2, JB, GL * N)
    scan = scan.transpose(2, 0, 1, 3).reshape(JB, levels, 2 * S5_BLOCK_STATE)
    return intra.astype(BF16), inject.astype(BF16), readout.astype(BF16), scan


def _s5_kernel(u_ref, intra_ref, inject_ref, readout_ref, scan_ref, y_ref, h_ref):
    rows = u_ref.shape[0]
    ns = S5_BLOCK_STATE

    @pl.when(pl.program_id(2) == 0)
    def _():
        h_ref[...] = jnp.zeros_like(h_ref)

    x = jnp.concatenate([u_ref[:, t, :] for t in range(S5_CHUNK)], axis=1).astype(BF16)
    e = _dot(x, inject_ref[0])
    er, ei = e[:, :ns], e[:, ns:]
    h_in = h_ref[...]
    hr, hi = h_in[:, :ns], h_in[:, ns:]
    row = lax.broadcasted_iota(jnp.int32, (rows, ns), 0)
    first = row == 0
    a = scan_ref[0]
    ar, ai = a[0:1, :ns], a[0:1, ns:]
    er = er + jnp.where(first, ar * hr - ai * hi, 0.0)
    ei = ei + jnp.where(first, ar * hi + ai * hr, 0.0)
    for k in range(a.shape[0]):
        d = 1 << k
        ar, ai = a[k:k + 1, :ns], a[k:k + 1, ns:]
        keep = row >= d
        sr = jnp.where(keep, pltpu.roll(er, d, axis=0), 0.0)
        si = jnp.where(keep, pltpu.roll(ei, d, axis=0), 0.0)
        er, ei = er + ar * sr - ai * si, ei + ar * si + ai * sr
    h_ref[...] = jnp.concatenate([er[rows - 1:rows], ei[rows - 1:rows]], axis=1)
    hx_r = jnp.where(first, hr, pltpu.roll(er, 1, axis=0))
    hx_i = jnp.where(first, hi, pltpu.roll(ei, 1, axis=0))
    hx = jnp.concatenate([hx_r, hx_i], axis=1).astype(BF16)
    y = _dot(x, intra_ref[0]) + _dot(hx, readout_ref[0])
    for t in range(S5_CHUNK):
        y_ref[:, t, :] = y[:, t * LANES:(t + 1) * LANES]


def s5_ssm(u, weights, batch, seq):
    intra, inject, readout, scan = weights
    rows_per_seq = seq // S5_CHUNK
    tile = min(S5_ROWS, rows_per_seq)
    assert tile == S5_ROWS, "sequence too short for the S5 row tile"
    nrt = rows_per_seq // tile
    u3 = u.reshape(batch * rows_per_seq, S5_CHUNK, SELF_WIDTH)
    wide = S5_CHUNK * LANES
    y3 = pl.pallas_call(
        _s5_kernel,
        grid=(batch, S5_LANE_BLOCKS, nrt),
        in_specs=[pl.BlockSpec((tile, S5_CHUNK, LANES), lambda b, j, r: (b * nrt + r, 0, j)),
                  pl.BlockSpec((1, wide, wide), lambda b, j, r: (j, 0, 0)),
                  pl.BlockSpec((1, wide, 2 * S5_BLOCK_STATE), lambda b, j, r: (j, 0, 0)),
                  pl.BlockSpec((1, 2 * S5_BLOCK_STATE, wide), lambda b, j, r: (j, 0, 0)),
                  pl.BlockSpec((1, scan.shape[1], 2 * S5_BLOCK_STATE), lambda b, j, r: (j, 0, 0))],
        out_specs=pl.BlockSpec((tile, S5_CHUNK, LANES), lambda b, j, r: (b * nrt + r, 0, j)),
        out_shape=jax.ShapeDtypeStruct(u3.shape, F32),
        scratch_shapes=[pltpu.VMEM((1, 2 * S5_BLOCK_STATE), F32)],
        compiler_params=_params(("parallel", "parallel", "arbitrary")),
        name="s5_ssm",
    )(u3, intra, inject, readout, scan)
    return y3.reshape(u.shape)


def _moba_kernel(q_ref, k_ref, vt_ref, kmh_ref, kml_ref, o_ref, qa_ref, qo_ref, m_ref, acc_ref,
                 sa_ref, sb_ref, ta_ref, tb_ref):
    j = pl.program_id(2)
    tq = q_ref.shape[0]
    blk_rows = MOBA_BLOCK
    q = q_ref[...].astype(F32)
    lane = lax.broadcasted_iota(jnp.int32, q.shape, 1)
    lo = lane < HEAD_DIM
    blk = jnp.where(lo, 1e9, (lane - HEAD_DIM).astype(F32))
    past = blk < j.astype(F32)
    ones = jnp.ones((MOBA_ONES_ROWS, blk_rows), BF16)
    kpos = lax.broadcasted_iota(jnp.int32, (blk_rows, 2 * tq), 0)
    qpos = lax.broadcasted_iota(jnp.int32, (blk_rows, 2 * tq), 1) % tq

    zeros = jnp.zeros(q.shape, F32)
    for e in range(2):
        keep = lo if e == 0 else jnp.logical_not(lo)
        qe = jnp.where(keep, q, 0.0)
        qe16 = qe.astype(BF16)
        gate = _dot_nt(qe16, kmh_ref[0, 0]) + _dot_nt(qe16, kml_ref[0, 0])
        g = jnp.where(past, gate, -jnp.inf)
        bias = jnp.full(gate.shape, NEG_INF, F32)
        for _ in range(MOBA_TOPK):
            top = jnp.max(g, axis=1, keepdims=True)
            idx = jnp.min(jnp.where(g == top, blk, 1e9), axis=1, keepdims=True)
            hit = blk == idx
            bias = jnp.where(hit, jnp.where(past, 0.0, NEG_INF), bias)
            g = jnp.where(hit, -jnp.inf, g)
        q_head = qe if e == 0 else pltpu.roll(qe, HEAD_DIM, axis=1)
        halves = [zeros, zeros]
        halves[e] = jnp.where(lo, q_head, bias)
        qa_ref[e * tq:(e + 1) * tq, :] = jnp.concatenate(halves, axis=1).astype(BF16)
        halves[e] = q_head
        qo_ref[e * tq:(e + 1) * tq, :] = jnp.concatenate(halves, axis=1).astype(BF16)

    def value_rows(n):
        return jnp.concatenate([vt_ref[n], ones], axis=0)

    s = _dot_nt(k_ref[0, pl.ds(pl.multiple_of(j * blk_rows, blk_rows), blk_rows), :], qo_ref[...])
    s = jnp.where(kpos <= qpos, s, NEG_INF)
    m0 = jnp.max(s, axis=0, keepdims=True)
    m_ref[...] = m0
    acc_ref[...] = _dot(value_rows(j), jnp.exp2(s - m0).astype(BF16))

    grp_rows = MOBA_GROUP * blk_rows
    ngroups = (j + MOBA_GROUP - 1) // MOBA_GROUP
    last_group = k_ref.shape[1] // grp_rows - 1

    def scores(grp, dst_ref, top_ref):
        kc = k_ref[0, pl.ds(pl.multiple_of(grp * grp_rows, grp_rows), grp_rows), :]
        s = _dot_nt(kc, qa_ref[...])
        dst_ref[...] = s
        top_ref[...] = jnp.max(s, axis=0, keepdims=True)

    def absorb(grp, src_ref, top_ref):
        m_old = m_ref[...]
        m_new = jnp.maximum(m_old, top_ref[...])
        pv = None
        for i in range(MOBA_GROUP):
            p = jnp.exp2(src_ref[i * blk_rows:(i + 1) * blk_rows, :] - m_new).astype(BF16)
            part = _dot(value_rows(grp * MOBA_GROUP + i), p)
            pv = part if pv is None else pv + part
        acc_ref[...] = jnp.exp2(m_old - m_new) * acc_ref[...] + pv
        m_ref[...] = m_new

    scores(0, sa_ref, ta_ref)

    def body(t, carry):
        scores(2 * t + 1, sb_ref, tb_ref)
        absorb(2 * t, sa_ref, ta_ref)
        scores(jnp.minimum(2 * t + 2, last_group), sa_ref, ta_ref)
        absorb(2 * t + 1, sb_ref, tb_ref)
        return carry

    lax.fori_loop(0, ngroups // 2, body, 0)

    @pl.when(ngroups % 2 == 1)
    def _():
        absorb(ngroups - 1, sa_ref, ta_ref)

    acc = acc_ref[...]
    out = acc[:2 * HEAD_DIM] / acc[2 * HEAD_DIM:2 * HEAD_DIM + 1]
    ot = jnp.concatenate([out[:HEAD_DIM, :tq], out[HEAD_DIM:, tq:]], axis=0)
    o_ref[...] = ot.T.astype(o_ref.dtype)


def moba_attention(q, kaug, vt3, kmh, kml, batch, seq):
    rows = q.shape[0]
    nb = seq // MOBA_BLOCK
    tq = MOBA_BLOCK
    assert nb % MOBA_GROUP == 0
    acc_rows = LANES + MOBA_ONES_ROWS
    return pl.pallas_call(
        _moba_kernel,
        grid=(batch, HEAD_PAIRS, nb),
        in_specs=[pl.BlockSpec((tq, LANES), lambda b, h, j: (b * nb + j, h)),
                  pl.BlockSpec((1, seq, 2 * LANES), lambda b, h, j: (b * HEAD_PAIRS + h, 0, 0)),
                  pl.BlockSpec((nb, LANES, MOBA_BLOCK), lambda b, h, j: (b, h, 0)),
                  pl.BlockSpec((1, 1, LANES, LANES), lambda b, h, j: (b, h, 0, 0)),
                  pl.BlockSpec((1, 1, LANES, LANES), lambda b, h, j: (b, h, 0, 0))],
        out_specs=pl.BlockSpec((tq, LANES), lambda b, h, j: (b * nb + j, h)),
        out_shape=jax.ShapeDtypeStruct((rows, SELF_WIDTH), BF16),
        scratch_shapes=[pltpu.VMEM((2 * tq, 2 * LANES), BF16),
                        pltpu.VMEM((2 * tq, 2 * LANES), BF16),
                        pltpu.VMEM((1, 2 * tq), F32),
                        pltpu.VMEM((acc_rows, 2 * tq), F32),
                        pltpu.VMEM((MOBA_GROUP * MOBA_BLOCK, 2 * tq), F32),
                        pltpu.VMEM((MOBA_GROUP * MOBA_BLOCK, 2 * tq), F32),
                        pltpu.VMEM((1, 2 * tq), F32),
                        pltpu.VMEM((1, 2 * tq), F32)],
        compiler_params=_params(("parallel", "parallel", "arbitrary")),
        name="moba_attention",
    )(q, kaug, vt3, kmh, kml)


def _key_mean_operand(kmean, batch, seq):
    nb = seq // MOBA_BLOCK
    km = kmean.reshape(batch, nb, HEAD_PAIRS, LANES).transpose(0, 2, 1, 3)
    km = jnp.pad(km, ((0, 0), (0, 0), (HEAD_DIM, LANES - HEAD_DIM - nb), (0, 0)))
    hi = km.astype(BF16)
    lo = (km - hi.astype(F32)).astype(BF16)
    return hi, lo


def _memory_heads(qm, km, vm):
    lane = lax.broadcasted_iota(jnp.int32, (qm.shape[0], LANES), 1)
    lo = lane < HEAD_DIM
    outs = []
    for pair in range(MEM_WIDTH // LANES):
        sl = slice(pair * LANES, (pair + 1) * LANES)
        qp, kp, vp = qm[:, sl], km[:, sl], vm[:, sl]
        zero = jnp.zeros_like(qp)
        res = []
        for keep in (lo, jnp.logical_not(lo)):
            s = _dot_nt(jnp.where(keep, qp, zero), kp) * ATTN_SCALE
            p = jnp.exp(s - jnp.max(s, axis=-1, keepdims=True))
            p = p / jnp.sum(p, axis=-1, keepdims=True)
            res.append(_dot(p.astype(BF16), vp))
        outs.append(jnp.where(lo, res[0], res[1]))
    return jnp.concatenate(outs, axis=1)


def _mixout_kernel(x_ref, s_ref, qm_ref, km_ref, vm_ref, *rest, glu):
    if glu:
        wg_ref, wo_ref, o_ref = rest
        y = jax.nn.gelu(s_ref[...])
        gate = _dot(y.astype(BF16), wg_ref[...])
        self_out = (y / (1.0 + jnp.exp(-gate))).astype(BF16)
    else:
        wo_ref, o_ref = rest
        self_out = s_ref[...]
    mem_out = _memory_heads(qm_ref[...], km_ref[...], vm_ref[...]).astype(BF16)
    mixed = jnp.concatenate([self_out, mem_out], axis=1)
    o_ref[...] = x_ref[...] + _dot(mixed, wo_ref[...])


def mix_out(x, self_pre, qm, mem_kv, w_glu, w_out, seq, tile=512):
    rows, d = x.shape
    n_mem = mem_kv.shape[0] // (rows // seq)
    per_seq = seq // tile
    glu = w_glu is not None
    in_specs = [pl.BlockSpec((tile, d), lambda i: (i, 0)),
                pl.BlockSpec((tile, SELF_WIDTH), lambda i: (i, 0)),
                pl.BlockSpec((tile, MEM_WIDTH), lambda i: (i, 0)),
                pl.BlockSpec((n_mem, MEM_WIDTH), lambda i: (i // per_seq, 0)),
                pl.BlockSpec((n_mem, MEM_WIDTH), lambda i: (i // per_seq, 1))]
    args = [x, self_pre, qm, mem_kv, mem_kv]
    if glu:
        in_specs.append(pl.BlockSpec((SELF_WIDTH, SELF_WIDTH), lambda i: (0, 0)))
        args.append(w_glu)
    in_specs.append(pl.BlockSpec((d, d), lambda i: (0, 0)))
    args.append(w_out)
    return pl.pallas_call(
        functools.partial(_mixout_kernel, glu=glu),
        grid=(rows // tile,),
        in_specs=in_specs,
        out_specs=pl.BlockSpec((tile, d), lambda i: (i, 0)),
        out_shape=jax.ShapeDtypeStruct((rows, d), F32),
        compiler_params=_params(("parallel",)),
        name="mix_out_glu" if glu else "mix_out",
    )(*args)


def _ffn_kernel(x_ref, halo_ref, g_ref, wu_ref, cw_ref, wd_ref, *rest, per_seq, final):
    if final:
        gf_ref, o_ref, act_ref = rest
    else:
        o_ref, act_ref = rest
    x = x_ref[...]
    gain = g_ref[...]
    halo = jnp.where(pl.program_id(0) % per_seq == 0, 0.0, halo_ref[...])
    hcat = jnp.concatenate([_rms(halo, gain).astype(BF16), _rms(x, gain).astype(BF16)], axis=0)

    def conv(u, w):
        return (w[0:1] * pltpu.roll(u, 2, axis=0)[FFN_HALO:] + w[1:2] * pltpu.roll(u, 1, axis=0)[FFN_HALO:]
                + w[2:3] * u[FFN_HALO:] + w[3:4])

    for c in range(D_FF // FFN_CHUNK):
        gate_cols = slice(c * FFN_CHUNK, (c + 1) * FFN_CHUNK)
        value_cols = slice(D_FF + c * FFN_CHUNK, D_FF + (c + 1) * FFN_CHUNK)
        gg = conv(_dot(hcat, wu_ref[:, gate_cols]), cw_ref[:, gate_cols])
        vv = conv(_dot(hcat, wu_ref[:, value_cols]), cw_ref[:, value_cols])
        act_ref[:, gate_cols] = (gg / (1.0 + jnp.exp(-gg)) * vv).astype(BF16)
    out = x + _dot(act_ref[...], wd_ref[...])
    if final:
        out = _rms(out, gf_ref[...])
    o_ref[...] = out


def conv_ffn(x, gain, w_up, conv_w, conv_b, w_down, seq, final_gain=None, tile=512):
    rows, d = x.shape
    per_seq = seq // tile
    cw = jnp.concatenate([conv_w, conv_b[None, :]], axis=0)
    final = final_gain is not None
    halo_blocks = tile // FFN_HALO
    in_specs = [pl.BlockSpec((tile, d), lambda i: (i, 0)),
                pl.BlockSpec((FFN_HALO, d), lambda i: (jnp.maximum(i * halo_blocks - 1, 0), 0)),
                pl.BlockSpec((1, d), lambda i: (0, 0)),
                pl.BlockSpec((d, 2 * D_FF), lambda i: (0, 0)),
                pl.BlockSpec((CONV_WIDTH + 1, 2 * D_FF), lambda i: (0, 0)),
                pl.BlockSpec((D_FF, d), lambda i: (0, 0))]
    args = [x, x, gain.reshape(1, d), w_up.astype(BF16), cw, w_down.astype(BF16)]
    if final:
        in_specs.append(pl.BlockSpec((1, d), lambda i: (0, 0)))
        args.append(final_gain.reshape(1, d))
    return pl.pallas_call(
        functools.partial(_ffn_kernel, per_seq=per_seq, final=final),
        grid=(rows // tile,),
        in_specs=in_specs,
        out_specs=pl.BlockSpec((tile, d), lambda i: (i, 0)),
        out_shape=jax.ShapeDtypeStruct((rows, d), F32),
        scratch_shapes=[pltpu.VMEM((tile, D_FF), BF16)],
        compiler_params=_params(("parallel",)),
        name="conv_ffn_final" if final else "conv_ffn",
    )(*args)


def _rope_tables(seq):
    pos = jnp.arange(seq, dtype=F32)
    inv = ROPE_THETA ** (-jnp.arange(0, HEAD_DIM, 2, dtype=F32) / HEAD_DIM)
    ang = pos[:, None] * inv[None, :]
    cos, sin = jnp.cos(ang), jnp.sin(ang)
    reps = LANES // HEAD_DIM
    cos_t = jnp.tile(jnp.concatenate([cos, cos], axis=1), (1, reps))
    sin_t = jnp.tile(jnp.concatenate([-sin, sin], axis=1), (1, reps))
    return cos_t, sin_t


def kernel(x, mem, ln_mix, w_in, w_out, mem_norm, w_mem_kv, ln_ffn, w_up, conv_w, conv_b, w_down,
           s5_lambda_re, s5_lambda_im, s5_log_step, s5_b_re, s5_b_im, s5_c_re, s5_c_im, s5_d,
           s5_w_glu, kv_norm, w_kv, final_norm):
    batch, seq, d = x.shape
    depth = ln_mix.shape[0]
    n_s5 = s5_lambda_re.shape[0]
    n_mem = mem.shape[1]
    xf = x.reshape(batch * seq, d)
    memf = mem.reshape(batch * n_mem, d)
    cos_t, sin_t = _rope_tables(seq)
    shared = None
    for l in range(depth):
        mem_kv = norm_matmul(memf, mem_norm[l], w_mem_kv[l].astype(BF16), n_mem, BF16)
        if l < n_s5:
            u, qm = inproj_s5(xf, ln_mix[l], w_in[l].astype(BF16))
            weights = _s5_weights(s5_lambda_re[l], s5_lambda_im[l], s5_log_step[l], s5_b_re[l], s5_b_im[l],
                                  s5_c_re[l], s5_c_im[l], s5_d[l])
            self_pre = s5_ssm(u, weights, batch, seq)
            w_glu = s5_w_glu[l].astype(BF16)
        else:
            q, qm, k3, vt3, kmean = inproj_moba(xf, ln_mix[l], kv_norm, w_in[l].astype(BF16),
                                                w_kv.astype(BF16), cos_t, sin_t, seq)
            if shared is None:
                shared = (k3, vt3, *_key_mean_operand(kmean, batch, seq))
            self_pre = moba_attention(q, *shared, batch, seq)
            w_glu = None
        xf = mix_out(xf, self_pre, qm, mem_kv, w_glu, w_out[l].astype(BF16), seq)
        xf = conv_ffn(xf, ln_ffn[l], w_up[l], conv_w[l], conv_b[l], w_down[l], seq,
                      final_gain=final_norm if l == depth - 1 else None)
    return xf.reshape(batch, seq, d)
```

```python
import functools
import math

import jax
import jax.numpy as jnp
import numpy as np
from jax import lax
from jax.experimental import pallas as pl
from jax.experimental.pallas import tpu as pltpu

F32 = jnp.float32
BF16 = jnp.bfloat16

D_MODEL = 1024
HEAD_DIM = 64
MEM_HEADS = 4
MEM_WIDTH = MEM_HEADS * HEAD_DIM
SELF_WIDTH = D_MODEL - MEM_WIDTH
S5_GROUP = 16
S5_GROUPS = SELF_WIDTH // S5_GROUP
S5_STATE = 64
MOBA_BLOCK = 256
MOBA_TOPK = 3
D_FF = 2816
CONV_WIDTH = 3
ROPE_THETA = 10000.0
NORM_EPS = 1e-6
NEG_INF = -1e30
ATTN_SCALE = HEAD_DIM ** -0.5

LOG2E = math.log2(math.e)
MOBA_GROUP = 4
MOBA_ONES_ROWS = 16

LANES = 128
HEAD_PAIRS = SELF_WIDTH // LANES
S5_CHUNK = 8
S5_LANE_BLOCKS = SELF_WIDTH // LANES
S5_GROUPS_PER_BLOCK = LANES // S5_GROUP
S5_BLOCK_STATE = S5_GROUPS_PER_BLOCK * S5_STATE
S5_ROWS = 256
FFN_CHUNK = 256
FFN_HALO = 16
VMEM_LIMIT = 56 * 1024 * 1024


def _params(sem, flags=None):
    return pltpu.CompilerParams(dimension_semantics=sem, vmem_limit_bytes=VMEM_LIMIT, flags=flags)


def _rms(x, gain):
    ms = jnp.mean(x * x, axis=-1, keepdims=True)
    return x * lax.rsqrt(ms + NORM_EPS) * gain


def _dot(a, b):
    return jnp.dot(a, b, preferred_element_type=F32)


def _dot_nt(a, b):
    return lax.dot_general(a, b, (((1,), (1,)), ((), ())), preferred_element_type=F32)


def _norm_matmul_kernel(x_ref, g_ref, w_ref, o_ref):
    h = _rms(x_ref[...], g_ref[...]).astype(BF16)
    o_ref[...] = _dot(h, w_ref[...]).astype(o_ref.dtype)


def norm_matmul(x, gain, w, tile, out_dtype):
    rows, d = x.shape
    n = w.shape[1]
    return pl.pallas_call(
        _norm_matmul_kernel,
        grid=(rows // tile,),
        in_specs=[pl.BlockSpec((tile, d), lambda i: (i, 0)),
                  pl.BlockSpec((1, d), lambda i: (0, 0)),
                  pl.BlockSpec((d, n), lambda i: (0, 0))],
        out_specs=pl.BlockSpec((tile, n), lambda i: (i, 0)),
        out_shape=jax.ShapeDtypeStruct((rows, n), out_dtype),
        compiler_params=_params(("parallel",)),
        name="norm_matmul",
    )(x, gain.reshape(1, d), w)


def _inproj_s5_kernel(x_ref, g_ref, w_ref, u_ref, qm_ref):
    h = _rms(x_ref[...], g_ref[...]).astype(BF16)
    z = _dot(h, w_ref[...])
    u_ref[...] = z[:, :SELF_WIDTH]
    qm_ref[...] = z[:, SELF_WIDTH:].astype(BF16)


def inproj_s5(x, gain, w, tile=512):
    rows, d = x.shape
    return pl.pallas_call(
        _inproj_s5_kernel,
        grid=(rows // tile,),
        in_specs=[pl.BlockSpec((tile, d), lambda i: (i, 0)),
                  pl.BlockSpec((1, d), lambda i: (0, 0)),
                  pl.BlockSpec((d, d), lambda i: (0, 0))],
        out_specs=[pl.BlockSpec((tile, SELF_WIDTH), lambda i: (i, 0)),
                   pl.BlockSpec((tile, MEM_WIDTH), lambda i: (i, 0))],
        out_shape=[jax.ShapeDtypeStruct((rows, SELF_WIDTH), F32),
                   jax.ShapeDtypeStruct((rows, MEM_WIDTH), BF16)],
        compiler_params=_params(("parallel",)),
        name="inproj_s5",
    )(x, gain.reshape(1, d), w)


def _rope(z, cos, sin_signed):
    lane = lax.broadcasted_iota(jnp.int32, (z.shape[0], LANES), 1)
    first_half = (lane % HEAD_DIM) < (HEAD_DIM // 2)
    outs = []
    for j in range(z.shape[1] // LANES):
        zj = z[:, j * LANES:(j + 1) * LANES]
        partner = jnp.where(first_half,
                            pltpu.roll(zj, LANES - HEAD_DIM // 2, axis=1),
                            pltpu.roll(zj, HEAD_DIM // 2, axis=1))
        outs.append(zj * cos + partner * sin_signed)
    return jnp.concatenate(outs, axis=1)


def _inproj_moba_kernel(x_ref, gq_ref, gkv_ref, wq_ref, wkv_ref, cos_ref, sin_ref,
                        q_ref, qm_ref, k_ref, vt_ref, km_ref, *, per_seq):
    x = x_ref[...]
    xn = x * lax.rsqrt(jnp.mean(x * x, axis=-1, keepdims=True) + NORM_EPS)
    cos, sin = cos_ref[...], sin_ref[...]
    z = _dot((xn * gq_ref[...]).astype(BF16), wq_ref[...])
    q_ref[...] = (_rope(z[:, :SELF_WIDTH], cos, sin) * (ATTN_SCALE * LOG2E)).astype(BF16)
    qm_ref[...] = z[:, SELF_WIDTH:].astype(BF16)
    kv = _dot((xn * gkv_ref[...]).astype(BF16), wkv_ref[...])
    k = _rope(kv[:, :SELF_WIDTH], cos, sin)
    km_ref[0] = jnp.mean(k, axis=0, keepdims=True)
    vt_ref[0] = kv[:, SELF_WIDTH:].T.astype(BF16)
    lane = lax.broadcasted_iota(jnp.int32, (x.shape[0], LANES), 1)
    lo = lane < HEAD_DIM
    onehot = jnp.where(lane - HEAD_DIM == pl.program_id(0) % per_seq, 1.0, 0.0)
    for pair in range(HEAD_PAIRS):
        kp = k[:, pair * LANES:(pair + 1) * LANES]
        k_ref[pair] = jnp.concatenate([jnp.where(lo, kp, onehot),
                                       jnp.where(lo, pltpu.roll(kp, HEAD_DIM, axis=1), onehot)], axis=1).astype(BF16)


def inproj_moba(x, gq, gkv, wq, wkv, cos, sin, seq):
    rows, d = x.shape
    tile = MOBA_BLOCK
    nblk = rows // tile
    per_seq = seq // tile
    assert per_seq <= LANES - HEAD_DIM, "block one-hot must fit beside the key in one lane tile"
    return pl.pallas_call(
        functools.partial(_inproj_moba_kernel, per_seq=per_seq),
        grid=(nblk,),
        in_specs=[pl.BlockSpec((tile, d), lambda i: (i, 0)),
                  pl.BlockSpec((1, d), lambda i: (0, 0)),
                  pl.BlockSpec((1, d), lambda i: (0, 0)),
                  pl.BlockSpec((d, d), lambda i: (0, 0)),
                  pl.BlockSpec((d, 2 * SELF_WIDTH), lambda i: (0, 0)),
                  pl.BlockSpec((tile, LANES), lambda i: (i % per_seq, 0)),
                  pl.BlockSpec((tile, LANES), lambda i: (i % per_seq, 0))],
        out_specs=[pl.BlockSpec((tile, SELF_WIDTH), lambda i: (i, 0)),
                   pl.BlockSpec((tile, MEM_WIDTH), lambda i: (i, 0)),
                   pl.BlockSpec((HEAD_PAIRS, tile, 2 * LANES), lambda i: (i // per_seq, i % per_seq, 0)),
                   pl.BlockSpec((1, SELF_WIDTH, tile), lambda i: (i, 0, 0)),
                   pl.BlockSpec((1, 1, SELF_WIDTH), lambda i: (i, 0, 0))],
        out_shape=[jax.ShapeDtypeStruct((rows, SELF_WIDTH), BF16),
                   jax.ShapeDtypeStruct((rows, MEM_WIDTH), BF16),
                   jax.ShapeDtypeStruct((rows // seq * HEAD_PAIRS, seq, 2 * LANES), BF16),
                   jax.ShapeDtypeStruct((nblk, SELF_WIDTH, tile), BF16),
                   jax.ShapeDtypeStruct((nblk, 1, SELF_WIDTH), F32)],
        compiler_params=_params(("parallel",)),
        name="inproj_moba",
    )(x, gq.reshape(1, d), gkv.reshape(1, d), wq, wkv, cos, sin)


def _s5_weights(lam_re, lam_im, log_step, b_re, b_im, c_re, c_im, d_skip):
    L = S5_CHUNK
    G, N, P = S5_GROUPS, S5_STATE, S5_GROUP
    JB, GL = S5_LANE_BLOCKS, S5_GROUPS_PER_BLOCK
    dt = jnp.exp(log_step)[:, None]
    mag = jnp.exp(lam_re * dt)
    ab_re, ab_im = mag * jnp.cos(lam_im * dt), mag * jnp.sin(lam_im * dt)
    den = lam_re * lam_re + lam_im * lam_im
    num_re, num_im = ab_re - 1.0, ab_im
    coef_re = (num_re * lam_re + num_im * lam_im) / den
    coef_im = (num_im * lam_re - num_re * lam_im) / den
    bb_re = coef_re[..., None] * b_re - coef_im[..., None] * b_im
    bb_im = coef_re[..., None] * b_im + coef_im[..., None] * b_re

    def power(j):
        j = jnp.asarray(j, F32).reshape((-1, 1, 1))
        m = jnp.exp(j * (lam_re * dt))
        return m * jnp.cos(j * (lam_im * dt)), m * jnp.sin(j * (lam_im * dt))

    pw_re, pw_im = power(jnp.arange(L + 1))
    cb_re = (jnp.einsum('gpn,dgn,gnq->dgpq', c_re, pw_re, bb_re)
             - jnp.einsum('gpn,dgn,gnq->dgpq', c_re, pw_im, bb_im)
             - jnp.einsum('gpn,dgn,gnq->dgpq', c_im, pw_re, bb_im)
             - jnp.einsum('gpn,dgn,gnq->dgpq', c_im, pw_im, bb_re))
    cb_re = cb_re.at[0].add(jnp.einsum('gp,pq->gpq', d_skip, jnp.eye(P, dtype=F32)))
    s_ix = jnp.arange(L)[:, None]
    t_ix = jnp.arange(L)[None, :]
    delta = jnp.clip(t_ix - s_ix, 0, L - 1)
    kst = jnp.where((t_ix >= s_ix)[..., None, None, None], cb_re[delta], 0.0)
    kst = kst.reshape(L, L, JB, GL, P, P)
    intra = kst.transpose(2, 0, 3, 5, 1, 4).reshape(JB, L * LANES, L * P)

    rev_re, rev_im = pw_re[L - 1 - jnp.arange(L)], pw_im[L - 1 - jnp.arange(L)]
    inj_re = rev_re[..., None] * bb_re[None] - rev_im[..., None] * bb_im[None]
    inj_im = rev_re[..., None] * bb_im[None] + rev_im[..., None] * bb_re[None]
    inj = jnp.stack([inj_re, inj_im], axis=0).reshape(2, L, JB, GL, N, P)
    inject = inj.transpose(2, 1, 3, 5, 0, 4).reshape(JB, L * LANES, 2 * N)

    nx_re, nx_im = pw_re[1:], pw_im[1:]
    ro_re = c_re[None] * nx_re[:, :, None, :] - c_im[None] * nx_im[:, :, None, :]
    ro_im = -(c_re[None] * nx_im[:, :, None, :] + c_im[None] * nx_re[:, :, None, :])
    ro = jnp.stack([ro_re, ro_im], axis=0).reshape(2, L, JB, GL, P, N)
    readout = ro.transpose(2, 0, 3, 5, 1, 4).reshape(JB, 2 * S5_BLOCK_STATE, L * P)

    levels = int(math.log2(S5_ROWS))
    sc_re, sc_im = power(L * (2 ** jnp.arange(levels)))
    scan = jnp.stack([sc_re, sc_im], axis=1).reshape(levels, 2, JB, GL * N)
    scan = scan.transpose(2, 0, 1, 3).reshape(JB, levels, 2 * S5_BLOCK_STATE)
    return intra.astype(BF16), inject.astype(BF16), readout.astype(BF16), scan


def _s5_spread_constants():
    L, P, N, GL = S5_CHUNK, S5_GROUP, S5_STATE, S5_GROUPS_PER_BLOCK
    wide = L * LANES
    col = np.arange(wide)
    tp_of_col = (col // LANES) * P + col % P
    cn_of_col = (col // (GL * N)) * N + col % N
    spread_tp = (np.arange(L * P)[:, None] == tp_of_col[None, :])
    spread_cn = (np.arange(2 * N)[:, None] == cn_of_col[None, :])
    g_tgp = (col % LANES) // P
    g_cgn = (col % (GL * N)) // N
    masks = np.stack([g_tgp[:, None] == g_tgp[None, :],
                      g_tgp[:, None] == g_cgn[None, :],
                      g_cgn[:, None] == g_tgp[None, :]])
    as_bf16 = lambda a: jnp.asarray(a.astype(np.float32), BF16)
    return as_bf16(spread_tp), as_bf16(spread_cn), as_bf16(masks)


def _s5_kernel(u_ref, intra_c_ref, inject_c_ref, readout_c_ref, scan_ref, spread_tp_ref, spread_cn_ref, mask_ref,
               y_ref, h_ref, intra_ref, inject_ref, readout_ref):
    rows = u_ref.shape[0]
    ns = S5_BLOCK_STATE

    @pl.when(pl.program_id(2) == 0)
    def _():
        h_ref[...] = jnp.zeros_like(h_ref)
        intra_ref[...] = (_dot(intra_c_ref[0], spread_tp_ref[...]) * mask_ref[0]).astype(BF16)
        inject_ref[...] = (_dot(inject_c_ref[0], spread_cn_ref[...]) * mask_ref[1]).astype(BF16)
        readout_ref[...] = (_dot(readout_c_ref[0], spread_tp_ref[...]) * mask_ref[2]).astype(BF16)

    x = jnp.concatenate([u_ref[:, t, :] for t in range(S5_CHUNK)], axis=1).astype(BF16)
    e = _dot(x, inject_ref[...])
    er, ei = e[:, :ns], e[:, ns:]
    h_in = h_ref[...]
    hr, hi = h_in[:, :ns], h_in[:, ns:]
    row = lax.broadcasted_iota(jnp.int32, (rows, ns), 0)
    first = row == 0
    a = scan_ref[0]
    ar, ai = a[0:1, :ns], a[0:1, ns:]
    er = er + jnp.where(first, ar * hr - ai * hi, 0.0)
    ei = ei + jnp.where(first, ar * hi + ai * hr, 0.0)
    for k in range(a.shape[0]):
        d = 1 << k
        ar, ai = a[k:k + 1, :ns], a[k:k + 1, ns:]
        keep = row >= d
        sr = jnp.where(keep, pltpu.roll(er, d, axis=0), 0.0)
        si = jnp.where(keep, pltpu.roll(ei, d, axis=0), 0.0)
        er, ei = er + ar * sr - ai * si, ei + ar * si + ai * sr
    h_ref[...] = jnp.concatenate([er[rows - 1:rows], ei[rows - 1:rows]], axis=1)
    hx_r = jnp.where(first, hr, pltpu.roll(er, 1, axis=0))
    hx_i = jnp.where(first, hi, pltpu.roll(ei, 1, axis=0))
    hx = jnp.concatenate([hx_r, hx_i], axis=1).astype(BF16)
    y = _dot(x, intra_ref[...]) + _dot(hx, readout_ref[...])
    for t in range(S5_CHUNK):
        y_ref[:, t, :] = y[:, t * LANES:(t + 1) * LANES]


def s5_ssm(u, weights, batch, seq):
    intra, inject, readout, scan = weights
    rows_per_seq = seq // S5_CHUNK
    tile = min(S5_ROWS, rows_per_seq)
    assert tile == S5_ROWS, "sequence too short for the S5 row tile"
    nrt = rows_per_seq // tile
    u3 = u.reshape(batch * rows_per_seq, S5_CHUNK, SELF_WIDTH)
    wide = S5_CHUNK * LANES
    states = 2 * S5_BLOCK_STATE
    assert wide == states, "one mask shape serves all three matrices"
    spread_tp, spread_cn, masks = _s5_spread_constants()
    compact = lambda a: pl.BlockSpec((1,) + a.shape[1:], lambda b, j, r: (j, 0, 0))
    whole = lambda a: pl.BlockSpec(a.shape, lambda b, j, r: (0,) * a.ndim)
    y3 = pl.pallas_call(
        _s5_kernel,
        grid=(batch, S5_LANE_BLOCKS, nrt),
        in_specs=[pl.BlockSpec((tile, S5_CHUNK, LANES), lambda b, j, r: (b * nrt + r, 0, j)),
                  compact(intra), compact(inject), compact(readout), compact(scan),
                  whole(spread_tp), whole(spread_cn), whole(masks)],
        out_specs=pl.BlockSpec((tile, S5_CHUNK, LANES), lambda b, j, r: (b * nrt + r, 0, j)),
        out_shape=jax.ShapeDtypeStruct(u3.shape, F32),
        scratch_shapes=[pltpu.VMEM((1, states), F32),
                        pltpu.VMEM((wide, wide), BF16),
                        pltpu.VMEM((wide, states), BF16),
                        pltpu.VMEM((states, wide), BF16)],
        compiler_params=_params(("parallel", "parallel", "arbitrary")),
        name="s5_ssm",
    )(u3, intra, inject, readout, scan, spread_tp, spread_cn, masks)
    return y3.reshape(u.shape)


def _moba_kernel(q_ref, k_ref, vt_ref, kmh_ref, kml_ref, o_ref, qa_ref, m_ref, acc_ref,
                 sa_ref, sb_ref, ta_ref, tb_ref):
    i = pl.program_id(2)
    tq = MOBA_BLOCK
    blk_rows = MOBA_BLOCK
    nt = MOBA_GROUP
    q = q_ref[...].astype(F32)
    lane = lax.broadcasted_iota(jnp.int32, q.shape, 1)
    row = lax.broadcasted_iota(jnp.int32, q.shape, 0)
    lo = lane < HEAD_DIM
    blk = jnp.where(lo, 1e9, (lane - HEAD_DIM).astype(F32))
    own_blk = (i * nt + row // tq).astype(F32)
    past = blk < own_blk
    own = blk == own_blk
    ones = jnp.ones((MOBA_ONES_ROWS, blk_rows), BF16)

    zeros = jnp.zeros((tq, LANES), F32)
    for e in range(2):
        keep = lo if e == 0 else jnp.logical_not(lo)
        qe = jnp.where(keep, q, 0.0)
        qe16 = qe.astype(BF16)
        gate = _dot_nt(qe16, kmh_ref[0, 0]) + _dot_nt(qe16, kml_ref[0, 0])
        g = jnp.where(past, gate, -jnp.inf)
        bias = jnp.where(own, 0.0, NEG_INF)
        for _ in range(MOBA_TOPK):
            top = jnp.max(g, axis=1, keepdims=True)
            idx = jnp.min(jnp.where(g == top, blk, 1e9), axis=1, keepdims=True)
            hit = blk == idx
            bias = jnp.where(hit & past, 0.0, bias)
            g = jnp.where(hit, -jnp.inf, g)
        q_head = qe if e == 0 else pltpu.roll(qe, HEAD_DIM, axis=1)
        placed = jnp.where(lo, q_head, bias)
        for t in range(nt):
            halves = [zeros, zeros]
            halves[e] = placed[t * tq:(t + 1) * tq]
            qa_ref[(2 * t + e) * tq:(2 * t + e + 1) * tq, :] = jnp.concatenate(halves, axis=1).astype(BF16)

    def value_rows(n):
        return jnp.concatenate([vt_ref[n], ones], axis=0)

    grp_rows = nt * blk_rows
    last_group = k_ref.shape[1] // grp_rows - 1

    def scores(grp, dst_ref, top_ref):
        kc = k_ref[0, pl.ds(pl.multiple_of(grp * grp_rows, grp_rows), grp_rows), :]
        s = _dot_nt(kc, qa_ref[...])
        dst_ref[...] = s
        top_ref[...] = jnp.max(s, axis=0, keepdims=True)

    sb_ref[...] = _dot_nt(k_ref[0, pl.ds(pl.multiple_of(i * grp_rows, grp_rows), grp_rows), :], qa_ref[...])
    kpos = lax.broadcasted_iota(jnp.int32, (blk_rows, 2 * tq), 0)
    qpos = lax.broadcasted_iota(jnp.int32, (blk_rows, 2 * tq), 1) % tq
    causal = jnp.where(kpos <= qpos, 0.0, NEG_INF)
    for t in range(nt):
        sb_ref[t * blk_rows:(t + 1) * blk_rows, 2 * t * tq:2 * (t + 1) * tq] += causal
    m0 = jnp.max(sb_ref[...], axis=0, keepdims=True)
    m_ref[...] = m0
    acc = None
    for b in range(nt):
        p = jnp.exp2(sb_ref[b * blk_rows:(b + 1) * blk_rows, :] - m0).astype(BF16)
        part = _dot(value_rows(i * nt + b), p)
        acc = part if acc is None else acc + part
    acc_ref[...] = acc
    ngroups = i

    def absorb(grp, src_ref, top_ref):
        m_old = m_ref[...]
        m_new = jnp.maximum(m_old, top_ref[...])
        pv = None
        for i in range(MOBA_GROUP):
            p = jnp.exp2(src_ref[i * blk_rows:(i + 1) * blk_rows, :] - m_new).astype(BF16)
            part = _dot(value_rows(grp * MOBA_GROUP + i), p)
            pv = part if pv is None else pv + part
        acc_ref[...] = jnp.exp2(m_old - m_new) * acc_ref[...] + pv
        m_ref[...] = m_new

    scores(0, sa_ref, ta_ref)

    def body(t, carry):
        scores(2 * t + 1, sb_ref, tb_ref)
        absorb(2 * t, sa_ref, ta_ref)
        scores(jnp.minimum(2 * t + 2, last_group), sa_ref, ta_ref)
        absorb(2 * t + 1, sb_ref, tb_ref)
        return carry

    lax.fori_loop(0, ngroups // 2, body, 0)

    @pl.when(ngroups % 2 == 1)
    def _():
        absorb(ngroups - 1, sa_ref, ta_ref)

    acc = acc_ref[...]
    out = acc[:2 * HEAD_DIM] / acc[2 * HEAD_DIM:2 * HEAD_DIM + 1]
    for t in range(nt):
        ot = jnp.concatenate([out[:HEAD_DIM, 2 * t * tq:(2 * t + 1) * tq],
                              out[HEAD_DIM:, (2 * t + 1) * tq:(2 * t + 2) * tq]], axis=0)
        o_ref[t * tq:(t + 1) * tq, :] = ot.T.astype(o_ref.dtype)


def moba_attention(q, kaug, vt3, kmh, kml, batch, seq):
    rows = q.shape[0]
    nb = seq // MOBA_BLOCK
    tq = MOBA_BLOCK
    assert nb % MOBA_GROUP == 0
    steps = nb // MOBA_GROUP
    q_rows = MOBA_GROUP * tq
    cols = 2 * q_rows
    acc_rows = LANES + MOBA_ONES_ROWS
    return pl.pallas_call(
        _moba_kernel,
        grid=(batch, HEAD_PAIRS, steps),
        in_specs=[pl.BlockSpec((q_rows, LANES), lambda b, h, i: (b * steps + i, h)),
                  pl.BlockSpec((1, seq, 2 * LANES), lambda b, h, i: (b * HEAD_PAIRS + h, 0, 0)),
                  pl.BlockSpec((nb, LANES, MOBA_BLOCK), lambda b, h, i: (b, h, 0)),
                  pl.BlockSpec((1, 1, LANES, LANES), lambda b, h, i: (b, h, 0, 0)),
                  pl.BlockSpec((1, 1, LANES, LANES), lambda b, h, i: (b, h, 0, 0))],
        out_specs=pl.BlockSpec((q_rows, LANES), lambda b, h, i: (b * steps + i, h)),
        out_shape=jax.ShapeDtypeStruct((rows, SELF_WIDTH), BF16),
        scratch_shapes=[pltpu.VMEM((cols, 2 * LANES), BF16),
                        pltpu.VMEM((1, cols), F32),
                        pltpu.VMEM((acc_rows, cols), F32),
                        pltpu.VMEM((MOBA_GROUP * MOBA_BLOCK, cols), F32),
                        pltpu.VMEM((MOBA_GROUP * MOBA_BLOCK, cols), F32),
                        pltpu.VMEM((1, cols), F32),
                        pltpu.VMEM((1, cols), F32)],
        compiler_params=_params(("parallel", "parallel", "arbitrary")),
        name="moba_attention",
    )(q, kaug, vt3, kmh, kml)


def _key_mean_operand(kmean, batch, seq):
    nb = seq // MOBA_BLOCK
    km = kmean.reshape(batch, nb, HEAD_PAIRS, LANES).transpose(0, 2, 1, 3)
    km = jnp.pad(km, ((0, 0), (0, 0), (HEAD_DIM, LANES - HEAD_DIM - nb), (0, 0)))
    hi = km.astype(BF16)
    lo = (km - hi.astype(F32)).astype(BF16)
    return hi, lo


def _memory_heads(qm, km, vm):
    lane = lax.broadcasted_iota(jnp.int32, (qm.shape[0], LANES), 1)
    lo = lane < HEAD_DIM
    outs = []
    for pair in range(MEM_WIDTH // LANES):
        sl = slice(pair * LANES, (pair + 1) * LANES)
        qp, kp, vp = qm[:, sl], km[:, sl], vm[:, sl]
        zero = jnp.zeros_like(qp)
        res = []
        for keep in (lo, jnp.logical_not(lo)):
            s = _dot_nt(jnp.where(keep, qp, zero), kp) * ATTN_SCALE
            p = jnp.exp(s - jnp.max(s, axis=-1, keepdims=True))
            p = p / jnp.sum(p, axis=-1, keepdims=True)
            res.append(_dot(p.astype(BF16), vp))
        outs.append(jnp.where(lo, res[0], res[1]))
    return jnp.concatenate(outs, axis=1)


def _mixout_kernel(x_ref, s_ref, qm_ref, km_ref, vm_ref, *rest, glu):
    if glu:
        wg_ref, wo_ref, o_ref = rest
        y = jax.nn.gelu(s_ref[...])
        gate = _dot(y.astype(BF16), wg_ref[...])
        self_out = (y / (1.0 + jnp.exp(-gate))).astype(BF16)
    else:
        wo_ref, o_ref = rest
        self_out = s_ref[...]
    mem_out = _memory_heads(qm_ref[...], km_ref[...], vm_ref[...]).astype(BF16)
    mixed = jnp.concatenate([self_out, mem_out], axis=1)
    o_ref[...] = x_ref[...] + _dot(mixed, wo_ref[...])


def mix_out(x, self_pre, qm, mem_kv, w_glu, w_out, seq, tile=512):
    rows, d = x.shape
    n_mem = mem_kv.shape[0] // (rows // seq)
    per_seq = seq // tile
    glu = w_glu is not None
    in_specs = [pl.BlockSpec((tile, d), lambda i: (i, 0)),
                pl.BlockSpec((tile, SELF_WIDTH), lambda i: (i, 0)),
                pl.BlockSpec((tile, MEM_WIDTH), lambda i: (i, 0)),
                pl.BlockSpec((n_mem, MEM_WIDTH), lambda i: (i // per_seq, 0)),
                pl.BlockSpec((n_mem, MEM_WIDTH), lambda i: (i // per_seq, 1))]
    args = [x, self_pre, qm, mem_kv, mem_kv]
    if glu:
        in_specs.append(pl.BlockSpec((SELF_WIDTH, SELF_WIDTH), lambda i: (0, 0)))
        args.append(w_glu)
    in_specs.append(pl.BlockSpec((d, d), lambda i: (0, 0)))
    args.append(w_out)
    return pl.pallas_call(
        functools.partial(_mixout_kernel, glu=glu),
        grid=(rows // tile,),
        in_specs=in_specs,
        out_specs=pl.BlockSpec((tile, d), lambda i: (i, 0)),
        out_shape=jax.ShapeDtypeStruct((rows, d), F32),
        compiler_params=_params(("parallel",)),
        name="mix_out_glu" if glu else "mix_out",
    )(*args)


def _ffn_kernel(x_ref, halo_ref, g_ref, wu_ref, cw_ref, wd_ref, *rest, per_seq, final):
    if final:
        gf_ref, o_ref, act_ref = rest
    else:
        o_ref, act_ref = rest
    x = x_ref[...]
    gain = g_ref[...]
    halo = jnp.where(pl.program_id(0) % per_seq == 0, 0.0, halo_ref[...])
    hcat = jnp.concatenate([_rms(halo, gain).astype(BF16), _rms(x, gain).astype(BF16)], axis=0)

    def conv(u, w):
        return (w[0:1] * pltpu.roll(u, 2, axis=0)[FFN_HALO:] + w[1:2] * pltpu.roll(u, 1, axis=0)[FFN_HALO:]
                + w[2:3] * u[FFN_HALO:] + w[3:4])

    for c in range(D_FF // FFN_CHUNK):
        gate_cols = slice(c * FFN_CHUNK, (c + 1) * FFN_CHUNK)
        value_cols = slice(D_FF + c * FFN_CHUNK, D_FF + (c + 1) * FFN_CHUNK)
        gg = conv(_dot(hcat, wu_ref[:, gate_cols]), cw_ref[:, gate_cols])
        vv = conv(_dot(hcat, wu_ref[:, value_cols]), cw_ref[:, value_cols])
        act_ref[:, gate_cols] = (gg / (1.0 + jnp.exp(-gg)) * vv).astype(BF16)
    out = x + _dot(act_ref[...], wd_ref[...])
    if final:
        out = _rms(out, gf_ref[...])
    o_ref[...] = out


def conv_ffn(x, gain, w_up, conv_w, conv_b, w_down, seq, final_gain=None, tile=512):
    rows, d = x.shape
    per_seq = seq // tile
    cw = jnp.concatenate([conv_w, conv_b[None, :]], axis=0)
    final = final_gain is not None
    halo_blocks = tile // FFN_HALO
    in_specs = [pl.BlockSpec((tile, d), lambda i: (i, 0)),
                pl.BlockSpec((FFN_HALO, d), lambda i: (jnp.maximum(i * halo_blocks - 1, 0), 0)),
                pl.BlockSpec((1, d), lambda i: (0, 0)),
                pl.BlockSpec((d, 2 * D_FF), lambda i: (0, 0)),
                pl.BlockSpec((CONV_WIDTH + 1, 2 * D_FF), lambda i: (0, 0)),
                pl.BlockSpec((D_FF, d), lambda i: (0, 0))]
    args = [x, x, gain.reshape(1, d), w_up.astype(BF16), cw, w_down.astype(BF16)]
    if final:
        in_specs.append(pl.BlockSpec((1, d), lambda i: (0, 0)))
        args.append(final_gain.reshape(1, d))
    return pl.pallas_call(
        functools.partial(_ffn_kernel, per_seq=per_seq, final=final),
        grid=(rows // tile,),
        in_specs=in_specs,
        out_specs=pl.BlockSpec((tile, d), lambda i: (i, 0)),
        out_shape=jax.ShapeDtypeStruct((rows, d), F32),
        scratch_shapes=[pltpu.VMEM((tile, D_FF), BF16)],
        compiler_params=_params(("parallel",)),
        name="conv_ffn_final" if final else "conv_ffn",
    )(*args)


def _rope_tables(seq):
    pos = jnp.arange(seq, dtype=F32)
    inv = ROPE_THETA ** (-jnp.arange(0, HEAD_DIM, 2, dtype=F32) / HEAD_DIM)
    ang = pos[:, None] * inv[None, :]
    cos, sin = jnp.cos(ang), jnp.sin(ang)
    reps = LANES // HEAD_DIM
    cos_t = jnp.tile(jnp.concatenate([cos, cos], axis=1), (1, reps))
    sin_t = jnp.tile(jnp.concatenate([-sin, sin], axis=1), (1, reps))
    return cos_t, sin_t


def kernel(x, mem, ln_mix, w_in, w_out, mem_norm, w_mem_kv, ln_ffn, w_up, conv_w, conv_b, w_down,
           s5_lambda_re, s5_lambda_im, s5_log_step, s5_b_re, s5_b_im, s5_c_re, s5_c_im, s5_d,
           s5_w_glu, kv_norm, w_kv, final_norm):
    batch, seq, d = x.shape
    depth = ln_mix.shape[0]
    n_s5 = s5_lambda_re.shape[0]
    n_mem = mem.shape[1]
    xf = x.reshape(batch * seq, d)
    memf = mem.reshape(batch * n_mem, d)
    cos_t, sin_t = _rope_tables(seq)
    shared = None
    for l in range(depth):
        mem_kv = norm_matmul(memf, mem_norm[l], w_mem_kv[l].astype(BF16), n_mem, BF16)
        if l < n_s5:
            u, qm = inproj_s5(xf, ln_mix[l], w_in[l].astype(BF16))
            weights = _s5_weights(s5_lambda_re[l], s5_lambda_im[l], s5_log_step[l], s5_b_re[l], s5_b_im[l],
                                  s5_c_re[l], s5_c_im[l], s5_d[l])
            self_pre = s5_ssm(u, weights, batch, seq)
            w_glu = s5_w_glu[l].astype(BF16)
        else:
            q, qm, k3, vt3, kmean = inproj_moba(xf, ln_mix[l], kv_norm, w_in[l].astype(BF16),
                                                w_kv.astype(BF16), cos_t, sin_t, seq)
            if shared is None:
                shared = (k3, vt3, *_key_mean_operand(kmean, batch, seq))
            self_pre = moba_attention(q, *shared, batch, seq)
            w_glu = None
        xf = mix_out(xf, self_pre, qm, mem_kv, w_glu, w_out[l].astype(BF16), seq)
        xf = conv_ffn(xf, ln_ffn[l], w_up[l], conv_w[l], conv_b[l], w_down[l], seq,
                      final_gain=final_norm if l == depth - 1 else None)
    return xf.reshape(batch, seq, d)
```

```python
import functools
import math

import jax
import jax.numpy as jnp
import numpy as np
from jax import lax
from jax.experimental import pallas as pl
from jax.experimental.pallas import tpu as pltpu

F32 = jnp.float32
BF16 = jnp.bfloat16

D_MODEL = 1024
HEAD_DIM = 64
MEM_HEADS = 4
MEM_WIDTH = MEM_HEADS * HEAD_DIM
SELF_WIDTH = D_MODEL - MEM_WIDTH
S5_GROUP = 16
S5_GROUPS = SELF_WIDTH // S5_GROUP
S5_STATE = 64
MOBA_BLOCK = 256
MOBA_TOPK = 3
D_FF = 2816
CONV_WIDTH = 3
ROPE_THETA = 10000.0
NORM_EPS = 1e-6
NEG_INF = -1e30
ATTN_SCALE = HEAD_DIM ** -0.5

LOG2E = math.log2(math.e)
MOBA_GROUP = 4
MOBA_ONES_ROWS = 16
NO_BLOCK = 1e9

LANES = 128
HEAD_PAIRS = SELF_WIDTH // LANES
S5_CHUNK = 8
S5_LANE_BLOCKS = SELF_WIDTH // LANES
S5_GROUPS_PER_BLOCK = LANES // S5_GROUP
S5_BLOCK_STATE = S5_GROUPS_PER_BLOCK * S5_STATE
S5_ROWS = 256
FFN_CHUNK = 256
FFN_HALO = 16
VMEM_LIMIT = 56 * 1024 * 1024


def _params(sem, flags=None):
    return pltpu.CompilerParams(dimension_semantics=sem, vmem_limit_bytes=VMEM_LIMIT, flags=flags)


def _rms(x, gain):
    ms = jnp.mean(x * x, axis=-1, keepdims=True)
    return x * lax.rsqrt(ms + NORM_EPS) * gain


def _dot(a, b):
    return jnp.dot(a, b, preferred_element_type=F32)


def _dot_nt(a, b):
    return lax.dot_general(a, b, (((1,), (1,)), ((), ())), preferred_element_type=F32)


def _norm_matmul_kernel(x_ref, g_ref, w_ref, o_ref):
    h = _rms(x_ref[...], g_ref[...]).astype(BF16)
    o_ref[...] = _dot(h, w_ref[...]).astype(o_ref.dtype)


def norm_matmul(x, gain, w, tile, out_dtype):
    rows, d = x.shape
    n = w.shape[1]
    return pl.pallas_call(
        _norm_matmul_kernel,
        grid=(rows // tile,),
        in_specs=[pl.BlockSpec((tile, d), lambda i: (i, 0)),
                  pl.BlockSpec((1, d), lambda i: (0, 0)),
                  pl.BlockSpec((d, n), lambda i: (0, 0))],
        out_specs=pl.BlockSpec((tile, n), lambda i: (i, 0)),
        out_shape=jax.ShapeDtypeStruct((rows, n), out_dtype),
        compiler_params=_params(("parallel",)),
        name="norm_matmul",
    )(x, gain.reshape(1, d), w)


def _inproj_s5_kernel(x_ref, g_ref, w_ref, u_ref, qm_ref):
    h = _rms(x_ref[...], g_ref[...]).astype(BF16)
    z = _dot(h, w_ref[...])
    u_ref[...] = z[:, :SELF_WIDTH]
    qm_ref[...] = z[:, SELF_WIDTH:].astype(BF16)


def inproj_s5(x, gain, w, tile=512):
    rows, d = x.shape
    return pl.pallas_call(
        _inproj_s5_kernel,
        grid=(rows // tile,),
        in_specs=[pl.BlockSpec((tile, d), lambda i: (i, 0)),
                  pl.BlockSpec((1, d), lambda i: (0, 0)),
                  pl.BlockSpec((d, d), lambda i: (0, 0))],
        out_specs=[pl.BlockSpec((tile, SELF_WIDTH), lambda i: (i, 0)),
                   pl.BlockSpec((tile, MEM_WIDTH), lambda i: (i, 0))],
        out_shape=[jax.ShapeDtypeStruct((rows, SELF_WIDTH), F32),
                   jax.ShapeDtypeStruct((rows, MEM_WIDTH), BF16)],
        compiler_params=_params(("parallel",)),
        name="inproj_s5",
    )(x, gain.reshape(1, d), w)


def _rope(z, cos, sin_signed):
    lane = lax.broadcasted_iota(jnp.int32, (z.shape[0], LANES), 1)
    first_half = (lane % HEAD_DIM) < (HEAD_DIM // 2)
    outs = []
    for j in range(z.shape[1] // LANES):
        zj = z[:, j * LANES:(j + 1) * LANES]
        partner = jnp.where(first_half,
                            pltpu.roll(zj, LANES - HEAD_DIM // 2, axis=1),
                            pltpu.roll(zj, HEAD_DIM // 2, axis=1))
        outs.append(zj * cos + partner * sin_signed)
    return jnp.concatenate(outs, axis=1)


def _inproj_moba_kernel(x_ref, gq_ref, gkv_ref, wq_ref, wkv_ref, cos_ref, sin_ref,
                        q_ref, qm_ref, k_ref, vt_ref, km_ref, *, per_seq):
    x = x_ref[...]
    xn = x * lax.rsqrt(jnp.mean(x * x, axis=-1, keepdims=True) + NORM_EPS)
    cos, sin = cos_ref[...], sin_ref[...]
    z = _dot((xn * gq_ref[...]).astype(BF16), wq_ref[...])
    q_ref[...] = (_rope(z[:, :SELF_WIDTH], cos, sin) * (ATTN_SCALE * LOG2E)).astype(BF16)
    qm_ref[...] = z[:, SELF_WIDTH:].astype(BF16)
    kv = _dot((xn * gkv_ref[...]).astype(BF16), wkv_ref[...])
    k = _rope(kv[:, :SELF_WIDTH], cos, sin)
    km_ref[0] = jnp.mean(k, axis=0, keepdims=True)
    vt_ref[0] = kv[:, SELF_WIDTH:].T.astype(BF16)
    lane = lax.broadcasted_iota(jnp.int32, (x.shape[0], LANES), 1)
    lo = lane < HEAD_DIM
    onehot = jnp.where(lane - HEAD_DIM == pl.program_id(0) % per_seq, 1.0, 0.0)
    for pair in range(HEAD_PAIRS):
        kp = k[:, pair * LANES:(pair + 1) * LANES]
        k_ref[pair] = jnp.concatenate([jnp.where(lo, kp, onehot),
                                       jnp.where(lo, pltpu.roll(kp, HEAD_DIM, axis=1), onehot)], axis=1).astype(BF16)


def inproj_moba(x, gq, gkv, wq, wkv, cos, sin, seq):
    rows, d = x.shape
    tile = MOBA_BLOCK
    nblk = rows // tile
    per_seq = seq // tile
    assert per_seq <= LANES - HEAD_DIM, "block one-hot must fit beside the key in one lane tile"
    return pl.pallas_call(
        functools.partial(_inproj_moba_kernel, per_seq=per_seq),
        grid=(nblk,),
        in_specs=[pl.BlockSpec((tile, d), lambda i: (i, 0)),
                  pl.BlockSpec((1, d), lambda i: (0, 0)),
                  pl.BlockSpec((1, d), lambda i: (0, 0)),
                  pl.BlockSpec((d, d), lambda i: (0, 0)),
                  pl.BlockSpec((d, 2 * SELF_WIDTH), lambda i: (0, 0)),
                  pl.BlockSpec((tile, LANES), lambda i: (i % per_seq, 0)),
                  pl.BlockSpec((tile, LANES), lambda i: (i % per_seq, 0))],
        out_specs=[pl.BlockSpec((tile, SELF_WIDTH), lambda i: (i, 0)),
                   pl.BlockSpec((tile, MEM_WIDTH), lambda i: (i, 0)),
                   pl.BlockSpec((HEAD_PAIRS, tile, 2 * LANES), lambda i: (i // per_seq, i % per_seq, 0)),
                   pl.BlockSpec((1, SELF_WIDTH, tile), lambda i: (i, 0, 0)),
                   pl.BlockSpec((1, 1, SELF_WIDTH), lambda i: (i, 0, 0))],
        out_shape=[jax.ShapeDtypeStruct((rows, SELF_WIDTH), BF16),
                   jax.ShapeDtypeStruct((rows, MEM_WIDTH), BF16),
                   jax.ShapeDtypeStruct((rows // seq * HEAD_PAIRS, seq, 2 * LANES), BF16),
                   jax.ShapeDtypeStruct((nblk, SELF_WIDTH, tile), BF16),
                   jax.ShapeDtypeStruct((nblk, 1, SELF_WIDTH), F32)],
        compiler_params=_params(("parallel",)),
        name="inproj_moba",
    )(x, gq.reshape(1, d), gkv.reshape(1, d), wq, wkv, cos, sin)


def _s5_weights(lam_re, lam_im, log_step, b_re, b_im, c_re, c_im, d_skip):
    L = S5_CHUNK
    G, N, P = S5_GROUPS, S5_STATE, S5_GROUP
    JB, GL = S5_LANE_BLOCKS, S5_GROUPS_PER_BLOCK
    dt = jnp.exp(log_step)[:, None]
    mag = jnp.exp(lam_re * dt)
    ab_re, ab_im = mag * jnp.cos(lam_im * dt), mag * jnp.sin(lam_im * dt)
    den = lam_re * lam_re + lam_im * lam_im
    num_re, num_im = ab_re - 1.0, ab_im
    coef_re = (num_re * lam_re + num_im * lam_im) / den
    coef_im = (num_im * lam_re - num_re * lam_im) / den
    bb_re = coef_re[..., None] * b_re - coef_im[..., None] * b_im
    bb_im = coef_re[..., None] * b_im + coef_im[..., None] * b_re

    def power(j):
        j = jnp.asarray(j, F32).reshape((-1, 1, 1))
        m = jnp.exp(j * (lam_re * dt))
        return m * jnp.cos(j * (lam_im * dt)), m * jnp.sin(j * (lam_im * dt))

    pw_re, pw_im = power(jnp.arange(L + 1))
    cb_re = (jnp.einsum('gpn,dgn,gnq->dgpq', c_re, pw_re, bb_re)
             - jnp.einsum('gpn,dgn,gnq->dgpq', c_re, pw_im, bb_im)
             - jnp.einsum('gpn,dgn,gnq->dgpq', c_im, pw_re, bb_im)
             - jnp.einsum('gpn,dgn,gnq->dgpq', c_im, pw_im, bb_re))
    cb_re = cb_re.at[0].add(jnp.einsum('gp,pq->gpq', d_skip, jnp.eye(P, dtype=F32)))
    s_ix = jnp.arange(L)[:, None]
    t_ix = jnp.arange(L)[None, :]
    delta = jnp.clip(t_ix - s_ix, 0, L - 1)
    kst = jnp.where((t_ix >= s_ix)[..., None, None, None], cb_re[delta], 0.0)
    kst = kst.reshape(L, L, JB, GL, P, P)
    intra = kst.transpose(2, 0, 3, 5, 1, 4).reshape(JB, L * LANES, L * P)

    rev_re, rev_im = pw_re[L - 1 - jnp.arange(L)], pw_im[L - 1 - jnp.arange(L)]
    inj_re = rev_re[..., None] * bb_re[None] - rev_im[..., None] * bb_im[None]
    inj_im = rev_re[..., None] * bb_im[None] + rev_im[..., None] * bb_re[None]
    inj = jnp.stack([inj_re, inj_im], axis=0).reshape(2, L, JB, GL, N, P)
    inject = inj.transpose(2, 1, 3, 5, 0, 4).reshape(JB, L * LANES, 2 * N)

    nx_re, nx_im = pw_re[1:], pw_im[1:]
    ro_re = c_re[None] * nx_re[:, :, None, :] - c_im[None] * nx_im[:, :, None, :]
    ro_im = -(c_re[None] * nx_im[:, :, None, :] + c_im[None] * nx_re[:, :, None, :])
    ro = jnp.stack([ro_re, ro_im], axis=0).reshape(2, L, JB, GL, P, N)
    readout = ro.transpose(2, 0, 3, 5, 1, 4).reshape(JB, 2 * S5_BLOCK_STATE, L * P)

    levels = int(math.log2(S5_ROWS))
    sc_re, sc_im = power(L * (2 ** jnp.arange(levels)))
    scan = jnp.stack([sc_re, sc_im], axis=1).reshape(levels, 2, JB, GL * N)
    scan = scan.transpose(2, 0, 1, 3).reshape(JB, levels, 2 * S5_BLOCK_STATE)
    return intra.astype(BF16), inject.astype(BF16), readout.astype(BF16), scan


def _s5_spread_constants():
    L, P, N, GL = S5_CHUNK, S5_GROUP, S5_STATE, S5_GROUPS_PER_BLOCK
    wide = L * LANES
    col = np.arange(wide)
    tp_of_col = (col // LANES) * P + col % P
    cn_of_col = (col // (GL * N)) * N + col % N
    spread_tp = (np.arange(L * P)[:, None] == tp_of_col[None, :])
    spread_cn = (np.arange(2 * N)[:, None] == cn_of_col[None, :])
    g_tgp = (col % LANES) // P
    g_cgn = (col % (GL * N)) // N
    masks = np.stack([g_tgp[:, None] == g_tgp[None, :],
                      g_tgp[:, None] == g_cgn[None, :],
                      g_cgn[:, None] == g_tgp[None, :]])
    as_bf16 = lambda a: jnp.asarray(a.astype(np.float32), BF16)
    return as_bf16(spread_tp), as_bf16(spread_cn), as_bf16(masks)


def _s5_kernel(u_ref, intra_c_ref, inject_c_ref, readout_c_ref, scan_ref, spread_tp_ref, spread_cn_ref, mask_ref,
               y_ref, h_ref, intra_ref, inject_ref, readout_ref):
    rows = u_ref.shape[0]
    ns = S5_BLOCK_STATE

    @pl.when(pl.program_id(2) == 0)
    def _():
        h_ref[...] = jnp.zeros_like(h_ref)
        intra_ref[...] = (_dot(intra_c_ref[0], spread_tp_ref[...]) * mask_ref[0]).astype(BF16)
        inject_ref[...] = (_dot(inject_c_ref[0], spread_cn_ref[...]) * mask_ref[1]).astype(BF16)
        readout_ref[...] = (_dot(readout_c_ref[0], spread_tp_ref[...]) * mask_ref[2]).astype(BF16)

    x = jnp.concatenate([u_ref[:, t, :] for t in range(S5_CHUNK)], axis=1).astype(BF16)
    e = _dot(x, inject_ref[...])
    er, ei = e[:, :ns], e[:, ns:]
    h_in = h_ref[...]
    hr, hi = h_in[:, :ns], h_in[:, ns:]
    row = lax.broadcasted_iota(jnp.int32, (rows, ns), 0)
    first = row == 0
    a = scan_ref[0]
    ar, ai = a[0:1, :ns], a[0:1, ns:]
    er = er + jnp.where(first, ar * hr - ai * hi, 0.0)
    ei = ei + jnp.where(first, ar * hi + ai * hr, 0.0)
    for k in range(a.shape[0]):
        d = 1 << k
        ar, ai = a[k:k + 1, :ns], a[k:k + 1, ns:]
        keep = row >= d
        sr = jnp.where(keep, pltpu.roll(er, d, axis=0), 0.0)
        si = jnp.where(keep, pltpu.roll(ei, d, axis=0), 0.0)
        er, ei = er + ar * sr - ai * si, ei + ar * si + ai * sr
    h_ref[...] = jnp.concatenate([er[rows - 1:rows], ei[rows - 1:rows]], axis=1)
    hx_r = jnp.where(first, hr, pltpu.roll(er, 1, axis=0))
    hx_i = jnp.where(first, hi, pltpu.roll(ei, 1, axis=0))
    hx = jnp.concatenate([hx_r, hx_i], axis=1).astype(BF16)
    y = _dot(x, intra_ref[...]) + _dot(hx, readout_ref[...])
    for t in range(S5_CHUNK):
        y_ref[:, t, :] = y[:, t * LANES:(t + 1) * LANES]


def s5_ssm(u, weights, batch, seq):
    intra, inject, readout, scan = weights
    rows_per_seq = seq // S5_CHUNK
    tile = min(S5_ROWS, rows_per_seq)
    assert tile == S5_ROWS, "sequence too short for the S5 row tile"
    nrt = rows_per_seq // tile
    u3 = u.reshape(batch * rows_per_seq, S5_CHUNK, SELF_WIDTH)
    wide = S5_CHUNK * LANES
    states = 2 * S5_BLOCK_STATE
    assert wide == states, "one mask shape serves all three matrices"
    spread_tp, spread_cn, masks = _s5_spread_constants()
    compact = lambda a: pl.BlockSpec((1,) + a.shape[1:], lambda b, j, r: (j, 0, 0))
    whole = lambda a: pl.BlockSpec(a.shape, lambda b, j, r: (0,) * a.ndim)
    y3 = pl.pallas_call(
        _s5_kernel,
        grid=(batch, S5_LANE_BLOCKS, nrt),
        in_specs=[pl.BlockSpec((tile, S5_CHUNK, LANES), lambda b, j, r: (b * nrt + r, 0, j)),
                  compact(intra), compact(inject), compact(readout), compact(scan),
                  whole(spread_tp), whole(spread_cn), whole(masks)],
        out_specs=pl.BlockSpec((tile, S5_CHUNK, LANES), lambda b, j, r: (b * nrt + r, 0, j)),
        out_shape=jax.ShapeDtypeStruct(u3.shape, F32),
        scratch_shapes=[pltpu.VMEM((1, states), F32),
                        pltpu.VMEM((wide, wide), BF16),
                        pltpu.VMEM((wide, states), BF16),
                        pltpu.VMEM((states, wide), BF16)],
        compiler_params=_params(("parallel", "parallel", "arbitrary")),
        name="s5_ssm",
    )(u3, intra, inject, readout, scan, spread_tp, spread_cn, masks)
    return y3.reshape(u.shape)


def _moba_query_operand(q_tile_ref, kmh_ref, kml_ref, step, dst_ref):
    tq = MOBA_BLOCK
    q = q_tile_ref[...].astype(F32)
    lane = lax.broadcasted_iota(jnp.int32, q.shape, 1)
    row = lax.broadcasted_iota(jnp.int32, q.shape, 0)
    lo = lane < HEAD_DIM
    blk = jnp.where(lo, NO_BLOCK, (lane - HEAD_DIM).astype(F32))
    own_blk = (step * MOBA_GROUP + row // tq).astype(F32)
    past = blk < own_blk
    own = blk == own_blk
    zeros = jnp.zeros((tq, LANES), F32)
    for e in range(2):
        keep = lo if e == 0 else jnp.logical_not(lo)
        qe = jnp.where(keep, q, 0.0)
        qe16 = qe.astype(BF16)
        gate = _dot_nt(qe16, kmh_ref[0, 0]) + _dot_nt(qe16, kml_ref[0, 0])
        g = jnp.where(past, gate, -jnp.inf)
        bias = jnp.where(own, 0.0, NEG_INF)
        for _ in range(MOBA_TOPK):
            top = jnp.max(g, axis=1, keepdims=True)
            idx = jnp.min(jnp.where(g == top, blk, NO_BLOCK), axis=1, keepdims=True)
            hit = blk == idx
            bias = jnp.where(hit & past, 0.0, bias)
            g = jnp.where(hit, -jnp.inf, g)
        q_head = qe if e == 0 else pltpu.roll(qe, HEAD_DIM, axis=1)
        placed = jnp.where(lo, q_head, bias)
        for t in range(MOBA_GROUP):
            halves = [zeros, zeros]
            halves[e] = placed[t * tq:(t + 1) * tq]
            dst_ref[(2 * t + e) * tq:(2 * t + e + 1) * tq, :] = jnp.concatenate(halves, axis=1).astype(BF16)


def _moba_kernel(q_ref, q_next_ref, k_ref, vt_ref, kmh_ref, kml_ref, o_ref, qa_ref, qa_next_ref, m_ref, acc_ref,
                 sa_ref, sb_ref, ta_ref, tb_ref):
    i = pl.program_id(2)
    tq = MOBA_BLOCK
    blk_rows = MOBA_BLOCK
    nt = MOBA_GROUP
    ones = jnp.ones((MOBA_ONES_ROWS, blk_rows), BF16)

    @pl.when(i == 0)
    def _():
        _moba_query_operand(q_ref, kmh_ref, kml_ref, 0, qa_ref)

    def value_rows(n):
        return jnp.concatenate([vt_ref[n], ones], axis=0)

    grp_rows = nt * blk_rows
    last_group = k_ref.shape[1] // grp_rows - 1

    def scores(grp, dst_ref, top_ref):
        kc = k_ref[0, pl.ds(pl.multiple_of(grp * grp_rows, grp_rows), grp_rows), :]
        s = _dot_nt(kc, qa_ref[...])
        dst_ref[...] = s
        top_ref[...] = jnp.max(s, axis=0, keepdims=True)

    _moba_query_operand(q_next_ref, kmh_ref, kml_ref, i + 1, qa_next_ref)

    sb_ref[...] = _dot_nt(k_ref[0, pl.ds(pl.multiple_of(i * grp_rows, grp_rows), grp_rows), :], qa_ref[...])
    kpos = lax.broadcasted_iota(jnp.int32, (blk_rows, 2 * tq), 0)
    qpos = lax.broadcasted_iota(jnp.int32, (blk_rows, 2 * tq), 1) % tq
    causal = jnp.where(kpos <= qpos, 0.0, NEG_INF)
    for t in range(nt):
        sb_ref[t * blk_rows:(t + 1) * blk_rows, 2 * t * tq:2 * (t + 1) * tq] += causal
    m0 = jnp.max(sb_ref[...], axis=0, keepdims=True)
    m_ref[...] = m0
    acc = None
    for b in range(nt):
        p = jnp.exp2(sb_ref[b * blk_rows:(b + 1) * blk_rows, :] - m0).astype(BF16)
        part = _dot(value_rows(i * nt + b), p)
        acc = part if acc is None else acc + part
    acc_ref[...] = acc
    ngroups = i

    def absorb(grp, src_ref, top_ref):
        m_old = m_ref[...]
        m_new = jnp.maximum(m_old, top_ref[...])
        pv = None
        for i in range(MOBA_GROUP):
            p = jnp.exp2(src_ref[i * blk_rows:(i + 1) * blk_rows, :] - m_new).astype(BF16)
            part = _dot(value_rows(grp * MOBA_GROUP + i), p)
            pv = part if pv is None else pv + part
        acc_ref[...] = jnp.exp2(m_old - m_new) * acc_ref[...] + pv
        m_ref[...] = m_new

    scores(0, sa_ref, ta_ref)

    def body(t, carry):
        scores(2 * t + 1, sb_ref, tb_ref)
        absorb(2 * t, sa_ref, ta_ref)
        scores(jnp.minimum(2 * t + 2, last_group), sa_ref, ta_ref)
        absorb(2 * t + 1, sb_ref, tb_ref)
        return carry

    lax.fori_loop(0, ngroups // 2, body, 0)

    @pl.when(ngroups % 2 == 1)
    def _():
        absorb(ngroups - 1, sa_ref, ta_ref)

    acc = acc_ref[...]
    out = acc[:2 * HEAD_DIM] / acc[2 * HEAD_DIM:2 * HEAD_DIM + 1]
    for t in range(nt):
        ot = jnp.concatenate([out[:HEAD_DIM, 2 * t * tq:(2 * t + 1) * tq],
                              out[HEAD_DIM:, (2 * t + 1) * tq:(2 * t + 2) * tq]], axis=0)
        o_ref[t * tq:(t + 1) * tq, :] = ot.T.astype(o_ref.dtype)
    qa_ref[...] = qa_next_ref[...]


def moba_attention(q, kaug, vt3, kmh, kml, batch, seq):
    rows = q.shape[0]
    nb = seq // MOBA_BLOCK
    tq = MOBA_BLOCK
    assert nb % MOBA_GROUP == 0
    steps = nb // MOBA_GROUP
    q_rows = MOBA_GROUP * tq
    cols = 2 * q_rows
    acc_rows = LANES + MOBA_ONES_ROWS
    return pl.pallas_call(
        _moba_kernel,
        grid=(batch, HEAD_PAIRS, steps),
        in_specs=[pl.BlockSpec((q_rows, LANES), lambda b, h, i: (b * steps + i, h)),
                  pl.BlockSpec((q_rows, LANES), lambda b, h, i: (b * steps + jnp.minimum(i + 1, steps - 1), h)),
                  pl.BlockSpec((1, seq, 2 * LANES), lambda b, h, i: (b * HEAD_PAIRS + h, 0, 0)),
                  pl.BlockSpec((nb, LANES, MOBA_BLOCK), lambda b, h, i: (b, h, 0)),
                  pl.BlockSpec((1, 1, LANES, LANES), lambda b, h, i: (b, h, 0, 0)),
                  pl.BlockSpec((1, 1, LANES, LANES), lambda b, h, i: (b, h, 0, 0))],
        out_specs=pl.BlockSpec((q_rows, LANES), lambda b, h, i: (b * steps + i, h)),
        out_shape=jax.ShapeDtypeStruct((rows, SELF_WIDTH), BF16),
        scratch_shapes=[pltpu.VMEM((cols, 2 * LANES), BF16),
                        pltpu.VMEM((cols, 2 * LANES), BF16),
                        pltpu.VMEM((1, cols), F32),
                        pltpu.VMEM((acc_rows, cols), F32),
                        pltpu.VMEM((MOBA_GROUP * MOBA_BLOCK, cols), F32),
                        pltpu.VMEM((MOBA_GROUP * MOBA_BLOCK, cols), F32),
                        pltpu.VMEM((1, cols), F32),
                        pltpu.VMEM((1, cols), F32)],
        compiler_params=_params(("parallel", "parallel", "arbitrary")),
        name="moba_attention",
    )(q, q, kaug, vt3, kmh, kml)


def _key_mean_operand(kmean, batch, seq):
    nb = seq // MOBA_BLOCK
    km = kmean.reshape(batch, nb, HEAD_PAIRS, LANES).transpose(0, 2, 1, 3)
    km = jnp.pad(km, ((0, 0), (0, 0), (HEAD_DIM, LANES - HEAD_DIM - nb), (0, 0)))
    hi = km.astype(BF16)
    lo = (km - hi.astype(F32)).astype(BF16)
    return hi, lo


def _memory_heads(qm, km, vm):
    lane = lax.broadcasted_iota(jnp.int32, (qm.shape[0], LANES), 1)
    lo = lane < HEAD_DIM
    outs = []
    for pair in range(MEM_WIDTH // LANES):
        sl = slice(pair * LANES, (pair + 1) * LANES)
        qp, kp, vp = qm[:, sl], km[:, sl], vm[:, sl]
        zero = jnp.zeros_like(qp)
        res = []
        for keep in (lo, jnp.logical_not(lo)):
            s = _dot_nt(jnp.where(keep, qp, zero), kp) * ATTN_SCALE
            p = jnp.exp(s - jnp.max(s, axis=-1, keepdims=True))
            p = p / jnp.sum(p, axis=-1, keepdims=True)
            res.append(_dot(p.astype(BF16), vp))
        outs.append(jnp.where(lo, res[0], res[1]))
    return jnp.concatenate(outs, axis=1)


def _mixout_kernel(x_ref, s_ref, qm_ref, km_ref, vm_ref, *rest, glu):
    if glu:
        wg_ref, wo_ref, o_ref = rest
        y = jax.nn.gelu(s_ref[...])
        gate = _dot(y.astype(BF16), wg_ref[...])
        self_out = (y / (1.0 + jnp.exp(-gate))).astype(BF16)
    else:
        wo_ref, o_ref = rest
        self_out = s_ref[...]
    mem_out = _memory_heads(qm_ref[...], km_ref[...], vm_ref[...]).astype(BF16)
    mixed = jnp.concatenate([self_out, mem_out], axis=1)
    o_ref[...] = x_ref[...] + _dot(mixed, wo_ref[...])


def mix_out(x, self_pre, qm, mem_kv, w_glu, w_out, seq, tile=512):
    rows, d = x.shape
    n_mem = mem_kv.shape[0] // (rows // seq)
    per_seq = seq // tile
    glu = w_glu is not None
    in_specs = [pl.BlockSpec((tile, d), lambda i: (i, 0)),
                pl.BlockSpec((tile, SELF_WIDTH), lambda i: (i, 0)),
                pl.BlockSpec((tile, MEM_WIDTH), lambda i: (i, 0)),
                pl.BlockSpec((n_mem, MEM_WIDTH), lambda i: (i // per_seq, 0)),
                pl.BlockSpec((n_mem, MEM_WIDTH), lambda i: (i // per_seq, 1))]
    args = [x, self_pre, qm, mem_kv, mem_kv]
    if glu:
        in_specs.append(pl.BlockSpec((SELF_WIDTH, SELF_WIDTH), lambda i: (0, 0)))
        args.append(w_glu)
    in_specs.append(pl.BlockSpec((d, d), lambda i: (0, 0)))
    args.append(w_out)
    return pl.pallas_call(
        functools.partial(_mixout_kernel, glu=glu),
        grid=(rows // tile,),
        in_specs=in_specs,
        out_specs=pl.BlockSpec((tile, d), lambda i: (i, 0)),
        out_shape=jax.ShapeDtypeStruct((rows, d), F32),
        compiler_params=_params(("parallel",)),
        name="mix_out_glu" if glu else "mix_out",
    )(*args)


def _ffn_kernel(x_ref, halo_ref, g_ref, wu_ref, cw_ref, wd_ref, *rest, per_seq, final):
    if final:
        gf_ref, o_ref, act_ref = rest
    else:
        o_ref, act_ref = rest
    x = x_ref[...]
    gain = g_ref[...]
    halo = jnp.where(pl.program_id(0) % per_seq == 0, 0.0, halo_ref[...])
    hcat = jnp.concatenate([_rms(halo, gain).astype(BF16), _rms(x, gain).astype(BF16)], axis=0)

    def conv(u, w):
        return (w[0:1] * pltpu.roll(u, 2, axis=0)[FFN_HALO:] + w[1:2] * pltpu.roll(u, 1, axis=0)[FFN_HALO:]
                + w[2:3] * u[FFN_HALO:] + w[3:4])

    for c in range(D_FF // FFN_CHUNK):
        gate_cols = slice(c * FFN_CHUNK, (c + 1) * FFN_CHUNK)
        value_cols = slice(D_FF + c * FFN_CHUNK, D_FF + (c + 1) * FFN_CHUNK)
        gg = conv(_dot(hcat, wu_ref[:, gate_cols]), cw_ref[:, gate_cols])
        vv = conv(_dot(hcat, wu_ref[:, value_cols]), cw_ref[:, value_cols])
        act_ref[:, gate_cols] = (gg / (1.0 + jnp.exp(-gg)) * vv).astype(BF16)
    out = x + _dot(act_ref[...], wd_ref[...])
    if final:
        out = _rms(out, gf_ref[...])
    o_ref[...] = out


def conv_ffn(x, gain, w_up, conv_w, conv_b, w_down, seq, final_gain=None, tile=512):
    rows, d = x.shape
    per_seq = seq // tile
    cw = jnp.concatenate([conv_w, conv_b[None, :]], axis=0)
    final = final_gain is not None
    halo_blocks = tile // FFN_HALO
    in_specs = [pl.BlockSpec((tile, d), lambda i: (i, 0)),
                pl.BlockSpec((FFN_HALO, d), lambda i: (jnp.maximum(i * halo_blocks - 1, 0), 0)),
                pl.BlockSpec((1, d), lambda i: (0, 0)),
                pl.BlockSpec((d, 2 * D_FF), lambda i: (0, 0)),
                pl.BlockSpec((CONV_WIDTH + 1, 2 * D_FF), lambda i: (0, 0)),
                pl.BlockSpec((D_FF, d), lambda i: (0, 0))]
    args = [x, x, gain.reshape(1, d), w_up.astype(BF16), cw, w_down.astype(BF16)]
    if final:
        in_specs.append(pl.BlockSpec((1, d), lambda i: (0, 0)))
        args.append(final_gain.reshape(1, d))
    return pl.pallas_call(
        functools.partial(_ffn_kernel, per_seq=per_seq, final=final),
        grid=(rows // tile,),
        in_specs=in_specs,
        out_specs=pl.BlockSpec((tile, d), lambda i: (i, 0)),
        out_shape=jax.ShapeDtypeStruct((rows, d), F32),
        scratch_shapes=[pltpu.VMEM((tile, D_FF), BF16)],
        compiler_params=_params(("parallel",)),
        name="conv_ffn_final" if final else "conv_ffn",
    )(*args)


def _rope_tables(seq):
    pos = jnp.arange(seq, dtype=F32)
    inv = ROPE_THETA ** (-jnp.arange(0, HEAD_DIM, 2, dtype=F32) / HEAD_DIM)
    ang = pos[:, None] * inv[None, :]
    cos, sin = jnp.cos(ang), jnp.sin(ang)
    reps = LANES // HEAD_DIM
    cos_t = jnp.tile(jnp.concatenate([cos, cos], axis=1), (1, reps))
    sin_t = jnp.tile(jnp.concatenate([-sin, sin], axis=1), (1, reps))
    return cos_t, sin_t


def kernel(x, mem, ln_mix, w_in, w_out, mem_norm, w_mem_kv, ln_ffn, w_up, conv_w, conv_b, w_down,
           s5_lambda_re, s5_lambda_im, s5_log_step, s5_b_re, s5_b_im, s5_c_re, s5_c_im, s5_d,
           s5_w_glu, kv_norm, w_kv, final_norm):
    batch, seq, d = x.shape
    depth = ln_mix.shape[0]
    n_s5 = s5_lambda_re.shape[0]
    n_mem = mem.shape[1]
    xf = x.reshape(batch * seq, d)
    memf = mem.reshape(batch * n_mem, d)
    cos_t, sin_t = _rope_tables(seq)
    shared = None
    for l in range(depth):
        mem_kv = norm_matmul(memf, mem_norm[l], w_mem_kv[l].astype(BF16), n_mem, BF16)
        if l < n_s5:
            u, qm = inproj_s5(xf, ln_mix[l], w_in[l].astype(BF16))
            weights = _s5_weights(s5_lambda_re[l], s5_lambda_im[l], s5_log_step[l], s5_b_re[l], s5_b_im[l],
                                  s5_c_re[l], s5_c_im[l], s5_d[l])
            self_pre = s5_ssm(u, weights, batch, seq)
            w_glu = s5_w_glu[l].astype(BF16)
        else:
            q, qm, k3, vt3, kmean = inproj_moba(xf, ln_mix[l], kv_norm, w_in[l].astype(BF16),
                                                w_kv.astype(BF16), cos_t, sin_t, seq)
            if shared is None:
                shared = (k3, vt3, *_key_mean_operand(kmean, batch, seq))
            self_pre = moba_attention(q, *shared, batch, seq)
            w_glu = None
        xf = mix_out(xf, self_pre, qm, mem_kv, w_glu, w_out[l].astype(BF16), seq)
        xf = conv_ffn(xf, ln_ffn[l], w_up[l], conv_w[l], conv_b[l], w_down[l], seq,
                      final_gain=final_norm if l == depth - 1 else None)
    return xf.reshape(batch, seq, d)
```

```python
import functools
import math

import jax
import jax.numpy as jnp
import numpy as np
from jax import lax
from jax.experimental import pallas as pl
from jax.experimental.pallas import tpu as pltpu

F32 = jnp.float32
BF16 = jnp.bfloat16

D_MODEL = 1024
HEAD_DIM = 64
MEM_HEADS = 4
MEM_WIDTH = MEM_HEADS * HEAD_DIM
SELF_WIDTH = D_MODEL - MEM_WIDTH
S5_GROUP = 16
S5_GROUPS = SELF_WIDTH // S5_GROUP
S5_STATE = 64
MOBA_BLOCK = 256
MOBA_TOPK = 3
D_FF = 2816
CONV_WIDTH = 3
ROPE_THETA = 10000.0
NORM_EPS = 1e-6
NEG_INF = -1e30
ATTN_SCALE = HEAD_DIM ** -0.5

LOG2E = math.log2(math.e)
MOBA_GROUP = 4
MOBA_ONES_ROWS = 16
NO_BLOCK = 1e9

LANES = 128
HEAD_PAIRS = SELF_WIDTH // LANES
S5_CHUNK = 8
S5_LANE_BLOCKS = SELF_WIDTH // LANES
S5_GROUPS_PER_BLOCK = LANES // S5_GROUP
S5_BLOCK_STATE = S5_GROUPS_PER_BLOCK * S5_STATE
S5_ROWS = 256
FFN_CHUNK = 256
FFN_HALO = 16
VMEM_LIMIT = 56 * 1024 * 1024


def _params(sem, flags=None):
    return pltpu.CompilerParams(dimension_semantics=sem, vmem_limit_bytes=VMEM_LIMIT, flags=flags)


def _rms(x, gain):
    ms = jnp.mean(x * x, axis=-1, keepdims=True)
    return x * lax.rsqrt(ms + NORM_EPS) * gain


def _dot(a, b):
    return jnp.dot(a, b, preferred_element_type=F32)


def _dot_nt(a, b):
    return lax.dot_general(a, b, (((1,), (1,)), ((), ())), preferred_element_type=F32)


def _norm_matmul_kernel(x_ref, g_ref, w_ref, o_ref):
    h = _rms(x_ref[...], g_ref[...]).astype(BF16)
    o_ref[...] = _dot(h, w_ref[...]).astype(o_ref.dtype)


def norm_matmul(x, gain, w, tile, out_dtype):
    rows, d = x.shape
    n = w.shape[1]
    return pl.pallas_call(
        _norm_matmul_kernel,
        grid=(rows // tile,),
        in_specs=[pl.BlockSpec((tile, d), lambda i: (i, 0)),
                  pl.BlockSpec((1, d), lambda i: (0, 0)),
                  pl.BlockSpec((d, n), lambda i: (0, 0))],
        out_specs=pl.BlockSpec((tile, n), lambda i: (i, 0)),
        out_shape=jax.ShapeDtypeStruct((rows, n), out_dtype),
        compiler_params=_params(("parallel",)),
        name="norm_matmul",
    )(x, gain.reshape(1, d), w)


def _inproj_s5_kernel(x_ref, g_ref, w_ref, u_ref, qm_ref):
    h = _rms(x_ref[...], g_ref[...]).astype(BF16)
    z = _dot(h, w_ref[...])
    u_ref[...] = z[:, :SELF_WIDTH]
    qm_ref[...] = z[:, SELF_WIDTH:].astype(BF16)


def inproj_s5(x, gain, w, tile=1024):
    rows, d = x.shape
    return pl.pallas_call(
        _inproj_s5_kernel,
        grid=(rows // tile,),
        in_specs=[pl.BlockSpec((tile, d), lambda i: (i, 0)),
                  pl.BlockSpec((1, d), lambda i: (0, 0)),
                  pl.BlockSpec((d, d), lambda i: (0, 0))],
        out_specs=[pl.BlockSpec((tile, SELF_WIDTH), lambda i: (i, 0)),
                   pl.BlockSpec((tile, MEM_WIDTH), lambda i: (i, 0))],
        out_shape=[jax.ShapeDtypeStruct((rows, SELF_WIDTH), F32),
                   jax.ShapeDtypeStruct((rows, MEM_WIDTH), BF16)],
        compiler_params=_params(("parallel",)),
        name="inproj_s5",
    )(x, gain.reshape(1, d), w)


def _rope(z, cos, sin_signed):
    lane = lax.broadcasted_iota(jnp.int32, (z.shape[0], LANES), 1)
    first_half = (lane % HEAD_DIM) < (HEAD_DIM // 2)
    outs = []
    for j in range(z.shape[1] // LANES):
        zj = z[:, j * LANES:(j + 1) * LANES]
        partner = jnp.where(first_half,
                            pltpu.roll(zj, LANES - HEAD_DIM // 2, axis=1),
                            pltpu.roll(zj, HEAD_DIM // 2, axis=1))
        outs.append(zj * cos + partner * sin_signed)
    return jnp.concatenate(outs, axis=1)


def _inproj_moba_kernel(x_ref, gq_ref, gkv_ref, wq_ref, wkv_ref, cos_ref, sin_ref,
                        q_ref, qm_ref, k_ref, vt_ref, km_ref, *, per_seq):
    x = x_ref[...]
    xn = x * lax.rsqrt(jnp.mean(x * x, axis=-1, keepdims=True) + NORM_EPS)
    cos, sin = cos_ref[...], sin_ref[...]
    z = _dot((xn * gq_ref[...]).astype(BF16), wq_ref[...])
    q_ref[...] = (_rope(z[:, :SELF_WIDTH], cos, sin) * (ATTN_SCALE * LOG2E)).astype(BF16)
    qm_ref[...] = z[:, SELF_WIDTH:].astype(BF16)
    kv = _dot((xn * gkv_ref[...]).astype(BF16), wkv_ref[...])
    k = _rope(kv[:, :SELF_WIDTH], cos, sin)
    km_ref[0] = jnp.mean(k, axis=0, keepdims=True)
    vt_ref[0] = kv[:, SELF_WIDTH:].T.astype(BF16)
    lane = lax.broadcasted_iota(jnp.int32, (x.shape[0], LANES), 1)
    lo = lane < HEAD_DIM
    onehot = jnp.where(lane - HEAD_DIM == pl.program_id(0) % per_seq, 1.0, 0.0)
    for pair in range(HEAD_PAIRS):
        kp = k[:, pair * LANES:(pair + 1) * LANES]
        k_ref[pair] = jnp.concatenate([jnp.where(lo, kp, onehot),
                                       jnp.where(lo, pltpu.roll(kp, HEAD_DIM, axis=1), onehot)], axis=1).astype(BF16)


def inproj_moba(x, gq, gkv, wq, wkv, cos, sin, seq):
    rows, d = x.shape
    tile = MOBA_BLOCK
    nblk = rows // tile
    per_seq = seq // tile
    assert per_seq <= LANES - HEAD_DIM, "block one-hot must fit beside the key in one lane tile"
    return pl.pallas_call(
        functools.partial(_inproj_moba_kernel, per_seq=per_seq),
        grid=(nblk,),
        in_specs=[pl.BlockSpec((tile, d), lambda i: (i, 0)),
                  pl.BlockSpec((1, d), lambda i: (0, 0)),
                  pl.BlockSpec((1, d), lambda i: (0, 0)),
                  pl.BlockSpec((d, d), lambda i: (0, 0)),
                  pl.BlockSpec((d, 2 * SELF_WIDTH), lambda i: (0, 0)),
                  pl.BlockSpec((tile, LANES), lambda i: (i % per_seq, 0)),
                  pl.BlockSpec((tile, LANES), lambda i: (i % per_seq, 0))],
        out_specs=[pl.BlockSpec((tile, SELF_WIDTH), lambda i: (i, 0)),
                   pl.BlockSpec((tile, MEM_WIDTH), lambda i: (i, 0)),
                   pl.BlockSpec((HEAD_PAIRS, tile, 2 * LANES), lambda i: (i // per_seq, i % per_seq, 0)),
                   pl.BlockSpec((1, SELF_WIDTH, tile), lambda i: (i, 0, 0)),
                   pl.BlockSpec((1, 1, SELF_WIDTH), lambda i: (i, 0, 0))],
        out_shape=[jax.ShapeDtypeStruct((rows, SELF_WIDTH), BF16),
                   jax.ShapeDtypeStruct((rows, MEM_WIDTH), BF16),
                   jax.ShapeDtypeStruct((rows // seq * HEAD_PAIRS, seq, 2 * LANES), BF16),
                   jax.ShapeDtypeStruct((nblk, SELF_WIDTH, tile), BF16),
                   jax.ShapeDtypeStruct((nblk, 1, SELF_WIDTH), F32)],
        compiler_params=_params(("parallel",)),
        name="inproj_moba",
    )(x, gq.reshape(1, d), gkv.reshape(1, d), wq, wkv, cos, sin)


def _s5_weights(lam_re, lam_im, log_step, b_re, b_im, c_re, c_im, d_skip):
    L = S5_CHUNK
    G, N, P = S5_GROUPS, S5_STATE, S5_GROUP
    JB, GL = S5_LANE_BLOCKS, S5_GROUPS_PER_BLOCK
    dt = jnp.exp(log_step)[:, None]
    mag = jnp.exp(lam_re * dt)
    ab_re, ab_im = mag * jnp.cos(lam_im * dt), mag * jnp.sin(lam_im * dt)
    den = lam_re * lam_re + lam_im * lam_im
    num_re, num_im = ab_re - 1.0, ab_im
    coef_re = (num_re * lam_re + num_im * lam_im) / den
    coef_im = (num_im * lam_re - num_re * lam_im) / den
    bb_re = coef_re[..., None] * b_re - coef_im[..., None] * b_im
    bb_im = coef_re[..., None] * b_im + coef_im[..., None] * b_re

    def power(j):
        j = jnp.asarray(j, F32).reshape((-1, 1, 1))
        m = jnp.exp(j * (lam_re * dt))
        return m * jnp.cos(j * (lam_im * dt)), m * jnp.sin(j * (lam_im * dt))

    pw_re, pw_im = power(jnp.arange(L + 1))
    cb_re = (jnp.einsum('gpn,dgn,gnq->dgpq', c_re, pw_re, bb_re)
             - jnp.einsum('gpn,dgn,gnq->dgpq', c_re, pw_im, bb_im)
             - jnp.einsum('gpn,dgn,gnq->dgpq', c_im, pw_re, bb_im)
             - jnp.einsum('gpn,dgn,gnq->dgpq', c_im, pw_im, bb_re))
    cb_re = cb_re.at[0].add(jnp.einsum('gp,pq->gpq', d_skip, jnp.eye(P, dtype=F32)))
    s_ix = jnp.arange(L)[:, None]
    t_ix = jnp.arange(L)[None, :]
    delta = jnp.clip(t_ix - s_ix, 0, L - 1)
    kst = jnp.where((t_ix >= s_ix)[..., None, None, None], cb_re[delta], 0.0)
    kst = kst.reshape(L, L, JB, GL, P, P)
    intra = kst.transpose(2, 0, 3, 5, 1, 4).reshape(JB, L * LANES, L * P)

    rev_re, rev_im = pw_re[L - 1 - jnp.arange(L)], pw_im[L - 1 - jnp.arange(L)]
    inj_re = rev_re[..., None] * bb_re[None] - rev_im[..., None] * bb_im[None]
    inj_im = rev_re[..., None] * bb_im[None] + rev_im[..., None] * bb_re[None]
    inj = jnp.stack([inj_re, inj_im], axis=0).reshape(2, L, JB, GL, N, P)
    inject = inj.transpose(2, 1, 3, 5, 0, 4).reshape(JB, L * LANES, 2 * N)

    nx_re, nx_im = pw_re[1:], pw_im[1:]
    ro_re = c_re[None] * nx_re[:, :, None, :] - c_im[None] * nx_im[:, :, None, :]
    ro_im = -(c_re[None] * nx_im[:, :, None, :] + c_im[None] * nx_re[:, :, None, :])
    ro = jnp.stack([ro_re, ro_im], axis=0).reshape(2, L, JB, GL, P, N)
    readout = ro.transpose(2, 0, 3, 5, 1, 4).reshape(JB, 2 * S5_BLOCK_STATE, L * P)

    levels = int(math.log2(S5_ROWS))
    sc_re, sc_im = power(L * (2 ** jnp.arange(levels)))
    scan = jnp.stack([sc_re, sc_im], axis=1).reshape(levels, 2, JB, GL * N)
    scan = scan.transpose(2, 0, 1, 3).reshape(JB, levels, 2 * S5_BLOCK_STATE)
    return intra.astype(BF16), inject.astype(BF16), readout.astype(BF16), scan


def _s5_spread_constants():
    L, P, N, GL = S5_CHUNK, S5_GROUP, S5_STATE, S5_GROUPS_PER_BLOCK
    wide = L * LANES
    col = np.arange(wide)
    tp_of_col = (col // LANES) * P + col % P
    cn_of_col = (col // (GL * N)) * N + col % N
    spread_tp = (np.arange(L * P)[:, None] == tp_of_col[None, :])
    spread_cn = (np.arange(2 * N)[:, None] == cn_of_col[None, :])
    g_tgp = (col % LANES) // P
    g_cgn = (col % (GL * N)) // N
    masks = np.stack([g_tgp[:, None] == g_tgp[None, :],
                      g_tgp[:, None] == g_cgn[None, :],
                      g_cgn[:, None] == g_tgp[None, :]])
    as_bf16 = lambda a: jnp.asarray(a.astype(np.float32), BF16)
    return as_bf16(spread_tp), as_bf16(spread_cn), as_bf16(masks)


def _s5_kernel(u_ref, intra_c_ref, inject_c_ref, readout_c_ref, scan_ref, spread_tp_ref, spread_cn_ref, mask_ref,
               y_ref, h_ref, intra_ref, inject_ref, readout_ref):
    rows = u_ref.shape[0]
    ns = S5_BLOCK_STATE

    @pl.when(pl.program_id(2) == 0)
    def _():
        h_ref[...] = jnp.zeros_like(h_ref)
        intra_ref[...] = (_dot(intra_c_ref[0], spread_tp_ref[...]) * mask_ref[0]).astype(BF16)
        inject_ref[...] = (_dot(inject_c_ref[0], spread_cn_ref[...]) * mask_ref[1]).astype(BF16)
        readout_ref[...] = (_dot(readout_c_ref[0], spread_tp_ref[...]) * mask_ref[2]).astype(BF16)

    x = jnp.concatenate([u_ref[:, t, :] for t in range(S5_CHUNK)], axis=1).astype(BF16)
    e = _dot(x, inject_ref[...])
    er, ei = e[:, :ns], e[:, ns:]
    h_in = h_ref[...]
    hr, hi = h_in[:, :ns], h_in[:, ns:]
    row = lax.broadcasted_iota(jnp.int32, (rows, ns), 0)
    first = row == 0
    a = scan_ref[0]
    ar, ai = a[0:1, :ns], a[0:1, ns:]
    er = er + jnp.where(first, ar * hr - ai * hi, 0.0)
    ei = ei + jnp.where(first, ar * hi + ai * hr, 0.0)
    for k in range(a.shape[0]):
        d = 1 << k
        ar, ai = a[k:k + 1, :ns], a[k:k + 1, ns:]
        keep = row >= d
        sr = jnp.where(keep, pltpu.roll(er, d, axis=0), 0.0)
        si = jnp.where(keep, pltpu.roll(ei, d, axis=0), 0.0)
        er, ei = er + ar * sr - ai * si, ei + ar * si + ai * sr
    h_ref[...] = jnp.concatenate([er[rows - 1:rows], ei[rows - 1:rows]], axis=1)
    hx_r = jnp.where(first, hr, pltpu.roll(er, 1, axis=0))
    hx_i = jnp.where(first, hi, pltpu.roll(ei, 1, axis=0))
    hx = jnp.concatenate([hx_r, hx_i], axis=1).astype(BF16)
    y = _dot(x, intra_ref[...]) + _dot(hx, readout_ref[...])
    for t in range(S5_CHUNK):
        y_ref[:, t, :] = y[:, t * LANES:(t + 1) * LANES]


def s5_ssm(u, weights, batch, seq):
    intra, inject, readout, scan = weights
    rows_per_seq = seq // S5_CHUNK
    tile = min(S5_ROWS, rows_per_seq)
    assert tile == S5_ROWS, "sequence too short for the S5 row tile"
    nrt = rows_per_seq // tile
    u3 = u.reshape(batch * rows_per_seq, S5_CHUNK, SELF_WIDTH)
    wide = S5_CHUNK * LANES
    states = 2 * S5_BLOCK_STATE
    assert wide == states, "one mask shape serves all three matrices"
    spread_tp, spread_cn, masks = _s5_spread_constants()
    compact = lambda a: pl.BlockSpec((1,) + a.shape[1:], lambda b, j, r: (j, 0, 0))
    whole = lambda a: pl.BlockSpec(a.shape, lambda b, j, r: (0,) * a.ndim)
    y3 = pl.pallas_call(
        _s5_kernel,
        grid=(batch, S5_LANE_BLOCKS, nrt),
        in_specs=[pl.BlockSpec((tile, S5_CHUNK, LANES), lambda b, j, r: (b * nrt + r, 0, j)),
                  compact(intra), compact(inject), compact(readout), compact(scan),
                  whole(spread_tp), whole(spread_cn), whole(masks)],
        out_specs=pl.BlockSpec((tile, S5_CHUNK, LANES), lambda b, j, r: (b * nrt + r, 0, j)),
        out_shape=jax.ShapeDtypeStruct(u3.shape, F32),
        scratch_shapes=[pltpu.VMEM((1, states), F32),
                        pltpu.VMEM((wide, wide), BF16),
                        pltpu.VMEM((wide, states), BF16),
                        pltpu.VMEM((states, wide), BF16)],
        compiler_params=_params(("parallel", "parallel", "arbitrary")),
        name="s5_ssm",
    )(u3, intra, inject, readout, scan, spread_tp, spread_cn, masks)
    return y3.reshape(u.shape)


def _moba_query_operand(q_tile_ref, kmh_ref, kml_ref, step, dst_ref):
    tq = MOBA_BLOCK
    q = q_tile_ref[...].astype(F32)
    lane = lax.broadcasted_iota(jnp.int32, q.shape, 1)
    row = lax.broadcasted_iota(jnp.int32, q.shape, 0)
    lo = lane < HEAD_DIM
    blk = jnp.where(lo, NO_BLOCK, (lane - HEAD_DIM).astype(F32))
    own_blk = (step * MOBA_GROUP + row // tq).astype(F32)
    past = blk < own_blk
    own = blk == own_blk
    zeros = jnp.zeros((tq, LANES), F32)
    for e in range(2):
        keep = lo if e == 0 else jnp.logical_not(lo)
        qe = jnp.where(keep, q, 0.0)
        qe16 = qe.astype(BF16)
        gate = _dot_nt(qe16, kmh_ref[0, 0]) + _dot_nt(qe16, kml_ref[0, 0])
        g = jnp.where(past, gate, -jnp.inf)
        bias = jnp.where(own, 0.0, NEG_INF)
        for _ in range(MOBA_TOPK):
            top = jnp.max(g, axis=1, keepdims=True)
            idx = jnp.min(jnp.where(g == top, blk, NO_BLOCK), axis=1, keepdims=True)
            hit = blk == idx
            bias = jnp.where(hit & past, 0.0, bias)
            g = jnp.where(hit, -jnp.inf, g)
        q_head = qe if e == 0 else pltpu.roll(qe, HEAD_DIM, axis=1)
        placed = jnp.where(lo, q_head, bias)
        for t in range(MOBA_GROUP):
            halves = [zeros, zeros]
            halves[e] = placed[t * tq:(t + 1) * tq]
            dst_ref[(2 * t + e) * tq:(2 * t + e + 1) * tq, :] = jnp.concatenate(halves, axis=1).astype(BF16)


def _moba_kernel(q_ref, q_next_ref, k_ref, vt_ref, kmh_ref, kml_ref, o_ref, qa_ref, qa_next_ref, m_ref, acc_ref,
                 sa_ref, sb_ref, ta_ref, tb_ref):
    i = pl.program_id(2)
    tq = MOBA_BLOCK
    blk_rows = MOBA_BLOCK
    nt = MOBA_GROUP
    ones = jnp.ones((MOBA_ONES_ROWS, blk_rows), BF16)

    @pl.when(i == 0)
    def _():
        _moba_query_operand(q_ref, kmh_ref, kml_ref, 0, qa_ref)

    def value_rows(n):
        return jnp.concatenate([vt_ref[n], ones], axis=0)

    grp_rows = nt * blk_rows
    last_group = k_ref.shape[1] // grp_rows - 1

    def scores(grp, dst_ref, top_ref):
        kc = k_ref[0, pl.ds(pl.multiple_of(grp * grp_rows, grp_rows), grp_rows), :]
        s = _dot_nt(kc, qa_ref[...])
        dst_ref[...] = s
        top_ref[...] = jnp.max(s, axis=0, keepdims=True)

    _moba_query_operand(q_next_ref, kmh_ref, kml_ref, i + 1, qa_next_ref)

    kpos = lax.broadcasted_iota(jnp.int32, (blk_rows, 2 * tq), 0)
    qpos = lax.broadcasted_iota(jnp.int32, (blk_rows, 2 * tq), 1) % tq
    causal = jnp.where(kpos <= qpos, 0.0, NEG_INF)
    diag_start = pl.multiple_of(i * grp_rows, grp_rows)
    for t in range(nt):
        cols = slice(2 * t * tq, 2 * (t + 1) * tq)
        keys = (t + 1) * blk_rows
        sb_ref[:keys, cols] = _dot_nt(k_ref[0, pl.ds(diag_start, keys), :], qa_ref[cols, :])
        sb_ref[t * blk_rows:keys, cols] += causal
        m0 = jnp.max(sb_ref[:keys, cols], axis=0, keepdims=True)
        pv = None
        for b in range(t + 1):
            p = jnp.exp2(sb_ref[b * blk_rows:(b + 1) * blk_rows, cols] - m0).astype(BF16)
            part = _dot(value_rows(i * nt + b), p)
            pv = part if pv is None else pv + part
        m_ref[:, cols] = m0
        acc_ref[:, cols] = pv
    ngroups = i

    def absorb(grp, src_ref, top_ref):
        m_old = m_ref[...]
        m_new = jnp.maximum(m_old, top_ref[...])
        pv = None
        for i in range(MOBA_GROUP):
            p = jnp.exp2(src_ref[i * blk_rows:(i + 1) * blk_rows, :] - m_new).astype(BF16)
            part = _dot(value_rows(grp * MOBA_GROUP + i), p)
            pv = part if pv is None else pv + part
        acc_ref[...] = jnp.exp2(m_old - m_new) * acc_ref[...] + pv
        m_ref[...] = m_new

    scores(0, sa_ref, ta_ref)

    def body(t, carry):
        scores(2 * t + 1, sb_ref, tb_ref)
        absorb(2 * t, sa_ref, ta_ref)
        scores(jnp.minimum(2 * t + 2, last_group), sa_ref, ta_ref)
        absorb(2 * t + 1, sb_ref, tb_ref)
        return carry

    lax.fori_loop(0, ngroups // 2, body, 0)

    @pl.when(ngroups % 2 == 1)
    def _():
        absorb(ngroups - 1, sa_ref, ta_ref)

    acc = acc_ref[...]
    out = acc[:2 * HEAD_DIM] / acc[2 * HEAD_DIM:2 * HEAD_DIM + 1]
    for t in range(nt):
        ot = jnp.concatenate([out[:HEAD_DIM, 2 * t * tq:(2 * t + 1) * tq],
                              out[HEAD_DIM:, (2 * t + 1) * tq:(2 * t + 2) * tq]], axis=0)
        o_ref[t * tq:(t + 1) * tq, :] = ot.T.astype(o_ref.dtype)
    qa_ref[...] = qa_next_ref[...]


def moba_attention(q, kaug, vt3, kmh, kml, batch, seq):
    rows = q.shape[0]
    nb = seq // MOBA_BLOCK
    tq = MOBA_BLOCK
    assert nb % MOBA_GROUP == 0
    steps = nb // MOBA_GROUP
    q_rows = MOBA_GROUP * tq
    cols = 2 * q_rows
    acc_rows = LANES + MOBA_ONES_ROWS
    return pl.pallas_call(
        _moba_kernel,
        grid=(batch, HEAD_PAIRS, steps),
        in_specs=[pl.BlockSpec((q_rows, LANES), lambda b, h, i: (b * steps + i, h)),
                  pl.BlockSpec((q_rows, LANES), lambda b, h, i: (b * steps + jnp.minimum(i + 1, steps - 1), h)),
                  pl.BlockSpec((1, seq, 2 * LANES), lambda b, h, i: (b * HEAD_PAIRS + h, 0, 0)),
                  pl.BlockSpec((nb, LANES, MOBA_BLOCK), lambda b, h, i: (b, h, 0)),
                  pl.BlockSpec((1, 1, LANES, LANES), lambda b, h, i: (b, h, 0, 0)),
                  pl.BlockSpec((1, 1, LANES, LANES), lambda b, h, i: (b, h, 0, 0))],
        out_specs=pl.BlockSpec((q_rows, LANES), lambda b, h, i: (b * steps + i, h)),
        out_shape=jax.ShapeDtypeStruct((rows, SELF_WIDTH), BF16),
        scratch_shapes=[pltpu.VMEM((cols, 2 * LANES), BF16),
                        pltpu.VMEM((cols, 2 * LANES), BF16),
                        pltpu.VMEM((1, cols), F32),
                        pltpu.VMEM((acc_rows, cols), F32),
                        pltpu.VMEM((MOBA_GROUP * MOBA_BLOCK, cols), F32),
                        pltpu.VMEM((MOBA_GROUP * MOBA_BLOCK, cols), F32),
                        pltpu.VMEM((1, cols), F32),
                        pltpu.VMEM((1, cols), F32)],
        compiler_params=_params(("parallel", "parallel", "arbitrary")),
        name="moba_attention",
    )(q, q, kaug, vt3, kmh, kml)


def _key_mean_operand(kmean, batch, seq):
    nb = seq // MOBA_BLOCK
    km = kmean.reshape(batch, nb, HEAD_PAIRS, LANES).transpose(0, 2, 1, 3)
    km = jnp.pad(km, ((0, 0), (0, 0), (HEAD_DIM, LANES - HEAD_DIM - nb), (0, 0)))
    hi = km.astype(BF16)
    lo = (km - hi.astype(F32)).astype(BF16)
    return hi, lo


def _memory_heads(qm, km, vm):
    lane = lax.broadcasted_iota(jnp.int32, (qm.shape[0], LANES), 1)
    lo = lane < HEAD_DIM
    outs = []
    for pair in range(MEM_WIDTH // LANES):
        sl = slice(pair * LANES, (pair + 1) * LANES)
        qp, kp, vp = qm[:, sl], km[:, sl], vm[:, sl]
        zero = jnp.zeros_like(qp)
        res = []
        for keep in (lo, jnp.logical_not(lo)):
            s = _dot_nt(jnp.where(keep, qp, zero), kp) * ATTN_SCALE
            p = jnp.exp(s - jnp.max(s, axis=-1, keepdims=True))
            p = p / jnp.sum(p, axis=-1, keepdims=True)
            res.append(_dot(p.astype(BF16), vp))
        outs.append(jnp.where(lo, res[0], res[1]))
    return jnp.concatenate(outs, axis=1)


def _mixout_kernel(x_ref, s_ref, qm_ref, km_ref, vm_ref, *rest, glu):
    if glu:
        wg_ref, wo_ref, o_ref = rest
        y = jax.nn.gelu(s_ref[...])
        gate = _dot(y.astype(BF16), wg_ref[...])
        self_out = (y / (1.0 + jnp.exp(-gate))).astype(BF16)
    else:
        wo_ref, o_ref = rest
        self_out = s_ref[...]
    mem_out = _memory_heads(qm_ref[...], km_ref[...], vm_ref[...]).astype(BF16)
    mixed = jnp.concatenate([self_out, mem_out], axis=1)
    o_ref[...] = x_ref[...] + _dot(mixed, wo_ref[...])


def mix_out(x, self_pre, qm, mem_kv, w_glu, w_out, seq, tile=1024):
    rows, d = x.shape
    n_mem = mem_kv.shape[0] // (rows // seq)
    per_seq = seq // tile
    glu = w_glu is not None
    in_specs = [pl.BlockSpec((tile, d), lambda i: (i, 0)),
                pl.BlockSpec((tile, SELF_WIDTH), lambda i: (i, 0)),
                pl.BlockSpec((tile, MEM_WIDTH), lambda i: (i, 0)),
                pl.BlockSpec((n_mem, MEM_WIDTH), lambda i: (i // per_seq, 0)),
                pl.BlockSpec((n_mem, MEM_WIDTH), lambda i: (i // per_seq, 1))]
    args = [x, self_pre, qm, mem_kv, mem_kv]
    if glu:
        in_specs.append(pl.BlockSpec((SELF_WIDTH, SELF_WIDTH), lambda i: (0, 0)))
        args.append(w_glu)
    in_specs.append(pl.BlockSpec((d, d), lambda i: (0, 0)))
    args.append(w_out)
    return pl.pallas_call(
        functools.partial(_mixout_kernel, glu=glu),
        grid=(rows // tile,),
        in_specs=in_specs,
        out_specs=pl.BlockSpec((tile, d), lambda i: (i, 0)),
        out_shape=jax.ShapeDtypeStruct((rows, d), F32),
        compiler_params=_params(("parallel",)),
        name="mix_out_glu" if glu else "mix_out",
    )(*args)


def _ffn_kernel(x_ref, halo_ref, g_ref, wu_ref, cw_ref, wd_ref, *rest, per_seq, final):
    if final:
        gf_ref, o_ref, act_ref = rest
    else:
        o_ref, act_ref = rest
    x = x_ref[...]
    gain = g_ref[...]
    halo = jnp.where(pl.program_id(0) % per_seq == 0, 0.0, halo_ref[...])
    hcat = jnp.concatenate([_rms(halo, gain).astype(BF16), _rms(x, gain).astype(BF16)], axis=0)

    def conv(u, w):
        return (w[0:1] * pltpu.roll(u, 2, axis=0)[FFN_HALO:] + w[1:2] * pltpu.roll(u, 1, axis=0)[FFN_HALO:]
                + w[2:3] * u[FFN_HALO:] + w[3:4])

    for c in range(D_FF // FFN_CHUNK):
        gate_cols = slice(c * FFN_CHUNK, (c + 1) * FFN_CHUNK)
        value_cols = slice(D_FF + c * FFN_CHUNK, D_FF + (c + 1) * FFN_CHUNK)
        gg = conv(_dot(hcat, wu_ref[:, gate_cols]), cw_ref[:, gate_cols])
        vv = conv(_dot(hcat, wu_ref[:, value_cols]), cw_ref[:, value_cols])
        act_ref[:, gate_cols] = (gg / (1.0 + jnp.exp(-gg)) * vv).astype(BF16)
    out = x + _dot(act_ref[...], wd_ref[...])
    if final:
        out = _rms(out, gf_ref[...])
    o_ref[...] = out


def conv_ffn(x, gain, w_up, conv_w, conv_b, w_down, seq, final_gain=None, tile=1024):
    rows, d = x.shape
    per_seq = seq // tile
    cw = jnp.concatenate([conv_w, conv_b[None, :]], axis=0)
    final = final_gain is not None
    halo_blocks = tile // FFN_HALO
    in_specs = [pl.BlockSpec((tile, d), lambda i: (i, 0)),
                pl.BlockSpec((FFN_HALO, d), lambda i: (jnp.maximum(i * halo_blocks - 1, 0), 0)),
                pl.BlockSpec((1, d), lambda i: (0, 0)),
                pl.BlockSpec((d, 2 * D_FF), lambda i: (0, 0), pipeline_mode=pl.Buffered(1)),
                pl.BlockSpec((CONV_WIDTH + 1, 2 * D_FF), lambda i: (0, 0)),
                pl.BlockSpec((D_FF, d), lambda i: (0, 0), pipeline_mode=pl.Buffered(1))]
    args = [x, x, gain.reshape(1, d), w_up.astype(BF16), cw, w_down.astype(BF16)]
    if final:
        in_specs.append(pl.BlockSpec((1, d), lambda i: (0, 0)))
        args.append(final_gain.reshape(1, d))
    return pl.pallas_call(
        functools.partial(_ffn_kernel, per_seq=per_seq, final=final),
        grid=(rows // tile,),
        in_specs=in_specs,
        out_specs=pl.BlockSpec((tile, d), lambda i: (i, 0)),
        out_shape=jax.ShapeDtypeStruct((rows, d), F32),
        scratch_shapes=[pltpu.VMEM((tile, D_FF), BF16)],
        compiler_params=_params(("parallel",)),
        name="conv_ffn_final" if final else "conv_ffn",
    )(*args)


def _rope_tables(seq):
    pos = jnp.arange(seq, dtype=F32)
    inv = ROPE_THETA ** (-jnp.arange(0, HEAD_DIM, 2, dtype=F32) / HEAD_DIM)
    ang = pos[:, None] * inv[None, :]
    cos, sin = jnp.cos(ang), jnp.sin(ang)
    reps = LANES // HEAD_DIM
    cos_t = jnp.tile(jnp.concatenate([cos, cos], axis=1), (1, reps))
    sin_t = jnp.tile(jnp.concatenate([-sin, sin], axis=1), (1, reps))
    return cos_t, sin_t


def kernel(x, mem, ln_mix, w_in, w_out, mem_norm, w_mem_kv, ln_ffn, w_up, conv_w, conv_b, w_down,
           s5_lambda_re, s5_lambda_im, s5_log_step, s5_b_re, s5_b_im, s5_c_re, s5_c_im, s5_d,
           s5_w_glu, kv_norm, w_kv, final_norm):
    batch, seq, d = x.shape
    depth = ln_mix.shape[0]
    n_s5 = s5_lambda_re.shape[0]
    n_mem = mem.shape[1]
    xf = x.reshape(batch * seq, d)
    memf = mem.reshape(batch * n_mem, d)
    cos_t, sin_t = _rope_tables(seq)
    shared = None
    for l in range(depth):
        mem_kv = norm_matmul(memf, mem_norm[l], w_mem_kv[l].astype(BF16), n_mem, BF16)
        if l < n_s5:
            u, qm = inproj_s5(xf, ln_mix[l], w_in[l].astype(BF16))
            weights = _s5_weights(s5_lambda_re[l], s5_lambda_im[l], s5_log_step[l], s5_b_re[l], s5_b_im[l],
                                  s5_c_re[l], s5_c_im[l], s5_d[l])
            self_pre = s5_ssm(u, weights, batch, seq)
            w_glu = s5_w_glu[l].astype(BF16)
        else:
            q, qm, k3, vt3, kmean = inproj_moba(xf, ln_mix[l], kv_norm, w_in[l].astype(BF16),
                                                w_kv.astype(BF16), cos_t, sin_t, seq)
            if shared is None:
                shared = (k3, vt3, *_key_mean_operand(kmean, batch, seq))
            self_pre = moba_attention(q, *shared, batch, seq)
            w_glu = None
        xf = mix_out(xf, self_pre, qm, mem_kv, w_glu, w_out[l].astype(BF16), seq)
        xf = conv_ffn(xf, ln_ffn[l], w_up[l], conv_w[l], conv_b[l], w_down[l], seq,
                      final_gain=final_norm if l == depth - 1 else None)
    return xf.reshape(batch, seq, d)
```

```python
import functools
import math

import jax
import jax.numpy as jnp
import numpy as np
from jax import lax
from jax.experimental import pallas as pl
from jax.experimental.pallas import tpu as pltpu

F32 = jnp.float32
BF16 = jnp.bfloat16

D_MODEL = 1024
HEAD_DIM = 64
MEM_HEADS = 4
MEM_WIDTH = MEM_HEADS * HEAD_DIM
SELF_WIDTH = D_MODEL - MEM_WIDTH
S5_GROUP = 16
S5_GROUPS = SELF_WIDTH // S5_GROUP
S5_STATE = 64
MOBA_BLOCK = 256
MOBA_TOPK = 3
D_FF = 2816
CONV_WIDTH = 3
ROPE_THETA = 10000.0
NORM_EPS = 1e-6
NEG_INF = -1e30
ATTN_SCALE = HEAD_DIM ** -0.5

LOG2E = math.log2(math.e)
MOBA_GROUP = 4
MOBA_ONES_ROWS = 16
NO_BLOCK = 1e9

LANES = 128
HEAD_PAIRS = SELF_WIDTH // LANES
S5_CHUNK = 8
S5_LANE_BLOCKS = SELF_WIDTH // LANES
S5_GROUPS_PER_BLOCK = LANES // S5_GROUP
S5_BLOCK_STATE = S5_GROUPS_PER_BLOCK * S5_STATE
S5_ROWS = 256
FFN_CHUNK = 256
FFN_HALO = 16
VMEM_LIMIT = 56 * 1024 * 1024


def _params(sem, flags=None):
    return pltpu.CompilerParams(dimension_semantics=sem, vmem_limit_bytes=VMEM_LIMIT, flags=flags)


def _rms(x, gain):
    ms = jnp.mean(x * x, axis=-1, keepdims=True)
    return x * lax.rsqrt(ms + NORM_EPS) * gain


def _dot(a, b):
    return jnp.dot(a, b, preferred_element_type=F32)


def _dot_nt(a, b):
    return lax.dot_general(a, b, (((1,), (1,)), ((), ())), preferred_element_type=F32)


def _norm_matmul_kernel(x_ref, g_ref, w_ref, o_ref):
    h = _rms(x_ref[...], g_ref[...]).astype(BF16)
    o_ref[...] = _dot(h, w_ref[...]).astype(o_ref.dtype)


def norm_matmul(x, gain, w, tile, out_dtype):
    rows, d = x.shape
    n = w.shape[1]
    return pl.pallas_call(
        _norm_matmul_kernel,
        grid=(rows // tile,),
        in_specs=[pl.BlockSpec((tile, d), lambda i: (i, 0)),
                  pl.BlockSpec((1, d), lambda i: (0, 0)),
                  pl.BlockSpec((d, n), lambda i: (0, 0))],
        out_specs=pl.BlockSpec((tile, n), lambda i: (i, 0)),
        out_shape=jax.ShapeDtypeStruct((rows, n), out_dtype),
        compiler_params=_params(("parallel",)),
        name="norm_matmul",
    )(x, gain.reshape(1, d), w)


def _inproj_s5_kernel(x_ref, g_ref, w_ref, u_ref, qm_ref):
    h = _rms(x_ref[...], g_ref[...]).astype(BF16)
    z = _dot(h, w_ref[...])
    u_ref[...] = z[:, :SELF_WIDTH]
    qm_ref[...] = z[:, SELF_WIDTH:].astype(BF16)


def inproj_s5(x, gain, w, tile=1024):
    rows, d = x.shape
    return pl.pallas_call(
        _inproj_s5_kernel,
        grid=(rows // tile,),
        in_specs=[pl.BlockSpec((tile, d), lambda i: (i, 0)),
                  pl.BlockSpec((1, d), lambda i: (0, 0)),
                  pl.BlockSpec((d, d), lambda i: (0, 0))],
        out_specs=[pl.BlockSpec((tile, SELF_WIDTH), lambda i: (i, 0)),
                   pl.BlockSpec((tile, MEM_WIDTH), lambda i: (i, 0))],
        out_shape=[jax.ShapeDtypeStruct((rows, SELF_WIDTH), F32),
                   jax.ShapeDtypeStruct((rows, MEM_WIDTH), BF16)],
        compiler_params=_params(("parallel",)),
        name="inproj_s5",
    )(x, gain.reshape(1, d), w)


def _rope(z, cos, sin_signed):
    lane = lax.broadcasted_iota(jnp.int32, (z.shape[0], LANES), 1)
    first_half = (lane % HEAD_DIM) < (HEAD_DIM // 2)
    outs = []
    for j in range(z.shape[1] // LANES):
        zj = z[:, j * LANES:(j + 1) * LANES]
        partner = jnp.where(first_half,
                            pltpu.roll(zj, LANES - HEAD_DIM // 2, axis=1),
                            pltpu.roll(zj, HEAD_DIM // 2, axis=1))
        outs.append(zj * cos + partner * sin_signed)
    return jnp.concatenate(outs, axis=1)


def _inproj_moba_kernel(x_ref, gq_ref, gkv_ref, wq_ref, wkv_ref, cos_ref, sin_ref,
                        q_ref, qm_ref, k_ref, vt_ref, km_ref, *, per_seq):
    x = x_ref[...]
    xn = x * lax.rsqrt(jnp.mean(x * x, axis=-1, keepdims=True) + NORM_EPS)
    cos, sin = cos_ref[...], sin_ref[...]
    z = _dot((xn * gq_ref[...]).astype(BF16), wq_ref[...])
    q_ref[...] = (_rope(z[:, :SELF_WIDTH], cos, sin) * (ATTN_SCALE * LOG2E)).astype(BF16)
    qm_ref[...] = z[:, SELF_WIDTH:].astype(BF16)
    kv = _dot((xn * gkv_ref[...]).astype(BF16), wkv_ref[...])
    k = _rope(kv[:, :SELF_WIDTH], cos, sin)
    km_ref[0] = jnp.mean(k, axis=0, keepdims=True)
    vt_ref[0] = kv[:, SELF_WIDTH:].T.astype(BF16)
    lane = lax.broadcasted_iota(jnp.int32, (x.shape[0], LANES), 1)
    lo = lane < HEAD_DIM
    onehot = jnp.where(lane - HEAD_DIM == pl.program_id(0) % per_seq, 1.0, 0.0)
    for pair in range(HEAD_PAIRS):
        kp = k[:, pair * LANES:(pair + 1) * LANES]
        k_ref[pair] = jnp.concatenate([jnp.where(lo, kp, onehot),
                                       jnp.where(lo, pltpu.roll(kp, HEAD_DIM, axis=1), onehot)], axis=1).astype(BF16)


def inproj_moba(x, gq, gkv, wq, wkv, cos, sin, seq):
    rows, d = x.shape
    tile = MOBA_BLOCK
    nblk = rows // tile
    per_seq = seq // tile
    assert per_seq <= LANES - HEAD_DIM, "block one-hot must fit beside the key in one lane tile"
    return pl.pallas_call(
        functools.partial(_inproj_moba_kernel, per_seq=per_seq),
        grid=(nblk,),
        in_specs=[pl.BlockSpec((tile, d), lambda i: (i, 0)),
                  pl.BlockSpec((1, d), lambda i: (0, 0)),
                  pl.BlockSpec((1, d), lambda i: (0, 0)),
                  pl.BlockSpec((d, d), lambda i: (0, 0)),
                  pl.BlockSpec((d, 2 * SELF_WIDTH), lambda i: (0, 0)),
                  pl.BlockSpec((tile, LANES), lambda i: (i % per_seq, 0)),
                  pl.BlockSpec((tile, LANES), lambda i: (i % per_seq, 0))],
        out_specs=[pl.BlockSpec((tile, SELF_WIDTH), lambda i: (i, 0)),
                   pl.BlockSpec((tile, MEM_WIDTH), lambda i: (i, 0)),
                   pl.BlockSpec((HEAD_PAIRS, tile, 2 * LANES), lambda i: (i // per_seq, i % per_seq, 0)),
                   pl.BlockSpec((1, SELF_WIDTH, tile), lambda i: (i, 0, 0)),
                   pl.BlockSpec((1, 1, SELF_WIDTH), lambda i: (i, 0, 0))],
        out_shape=[jax.ShapeDtypeStruct((rows, SELF_WIDTH), BF16),
                   jax.ShapeDtypeStruct((rows, MEM_WIDTH), BF16),
                   jax.ShapeDtypeStruct((rows // seq * HEAD_PAIRS, seq, 2 * LANES), BF16),
                   jax.ShapeDtypeStruct((nblk, SELF_WIDTH, tile), BF16),
                   jax.ShapeDtypeStruct((nblk, 1, SELF_WIDTH), F32)],
        compiler_params=_params(("parallel",)),
        name="inproj_moba",
    )(x, gq.reshape(1, d), gkv.reshape(1, d), wq, wkv, cos, sin)


def _s5_weights(lam_re, lam_im, log_step, b_re, b_im, c_re, c_im, d_skip):
    L = S5_CHUNK
    G, N, P = S5_GROUPS, S5_STATE, S5_GROUP
    JB, GL = S5_LANE_BLOCKS, S5_GROUPS_PER_BLOCK
    dt = jnp.exp(log_step)[:, None]
    mag = jnp.exp(lam_re * dt)
    ab_re, ab_im = mag * jnp.cos(lam_im * dt), mag * jnp.sin(lam_im * dt)
    den = lam_re * lam_re + lam_im * lam_im
    num_re, num_im = ab_re - 1.0, ab_im
    coef_re = (num_re * lam_re + num_im * lam_im) / den
    coef_im = (num_im * lam_re - num_re * lam_im) / den
    bb_re = coef_re[..., None] * b_re - coef_im[..., None] * b_im
    bb_im = coef_re[..., None] * b_im + coef_im[..., None] * b_re

    def power(j):
        j = jnp.asarray(j, F32).reshape((-1, 1, 1))
        m = jnp.exp(j * (lam_re * dt))
        return m * jnp.cos(j * (lam_im * dt)), m * jnp.sin(j * (lam_im * dt))

    pw_re, pw_im = power(jnp.arange(L + 1))
    cb_re = (jnp.einsum('gpn,dgn,gnq->dgpq', c_re, pw_re, bb_re)
             - jnp.einsum('gpn,dgn,gnq->dgpq', c_re, pw_im, bb_im)
             - jnp.einsum('gpn,dgn,gnq->dgpq', c_im, pw_re, bb_im)
             - jnp.einsum('gpn,dgn,gnq->dgpq', c_im, pw_im, bb_re))
    cb_re = cb_re.at[0].add(jnp.einsum('gp,pq->gpq', d_skip, jnp.eye(P, dtype=F32)))
    s_ix = jnp.arange(L)[:, None]
    t_ix = jnp.arange(L)[None, :]
    delta = jnp.clip(t_ix - s_ix, 0, L - 1)
    kst = jnp.where((t_ix >= s_ix)[..., None, None, None], cb_re[delta], 0.0)
    kst = kst.reshape(L, L, JB, GL, P, P)
    intra = kst.transpose(2, 0, 3, 5, 1, 4).reshape(JB, L * LANES, L * P)

    rev_re, rev_im = pw_re[L - 1 - jnp.arange(L)], pw_im[L - 1 - jnp.arange(L)]
    inj_re = rev_re[..., None] * bb_re[None] - rev_im[..., None] * bb_im[None]
    inj_im = rev_re[..., None] * bb_im[None] + rev_im[..., None] * bb_re[None]
    inj = jnp.stack([inj_re, inj_im], axis=0).reshape(2, L, JB, GL, N, P)
    inject = inj.transpose(2, 1, 3, 5, 0, 4).reshape(JB, L * LANES, 2 * N)

    nx_re, nx_im = pw_re[1:], pw_im[1:]
    ro_re = c_re[None] * nx_re[:, :, None, :] - c_im[None] * nx_im[:, :, None, :]
    ro_im = -(c_re[None] * nx_im[:, :, None, :] + c_im[None] * nx_re[:, :, None, :])
    ro = jnp.stack([ro_re, ro_im], axis=0).reshape(2, L, JB, GL, P, N)
    readout = ro.transpose(2, 0, 3, 5, 1, 4).reshape(JB, 2 * S5_BLOCK_STATE, L * P)

    levels = int(math.log2(S5_ROWS))
    sc_re, sc_im = power(L * (2 ** jnp.arange(levels)))
    scan = jnp.stack([sc_re, sc_im], axis=1).reshape(levels, 2, JB, GL * N)
    scan = scan.transpose(2, 0, 1, 3).reshape(JB, levels, 2 * S5_BLOCK_STATE)
    return intra.astype(BF16), inject.astype(BF16), readout.astype(BF16), scan


def _s5_spread_constants():
    L, P, N, GL = S5_CHUNK, S5_GROUP, S5_STATE, S5_GROUPS_PER_BLOCK
    wide = L * LANES
    col = np.arange(wide)
    tp_of_col = (col // LANES) * P + col % P
    cn_of_col = (col // (GL * N)) * N + col % N
    spread_tp = (np.arange(L * P)[:, None] == tp_of_col[None, :])
    spread_cn = (np.arange(2 * N)[:, None] == cn_of_col[None, :])
    g_tgp = (col % LANES) // P
    g_cgn = (col % (GL * N)) // N
    masks = np.stack([g_tgp[:, None] == g_tgp[None, :],
                      g_tgp[:, None] == g_cgn[None, :],
                      g_cgn[:, None] == g_tgp[None, :]])
    as_bf16 = lambda a: jnp.asarray(a.astype(np.float32), BF16)
    return as_bf16(spread_tp), as_bf16(spread_cn), as_bf16(masks)


def _s5_kernel(u_ref, intra_c_ref, inject_c_ref, readout_c_ref, scan_ref, spread_tp_ref, spread_cn_ref, mask_ref,
               y_ref, h_ref, intra_ref, inject_ref, readout_ref):
    rows = u_ref.shape[0]
    ns = S5_BLOCK_STATE

    @pl.when(pl.program_id(2) == 0)
    def _():
        h_ref[...] = jnp.zeros_like(h_ref)
        intra_ref[...] = (_dot(intra_c_ref[0], spread_tp_ref[...]) * mask_ref[0]).astype(BF16)
        inject_ref[...] = (_dot(inject_c_ref[0], spread_cn_ref[...]) * mask_ref[1]).astype(BF16)
        readout_ref[...] = (_dot(readout_c_ref[0], spread_tp_ref[...]) * mask_ref[2]).astype(BF16)

    x = jnp.concatenate([u_ref[:, t, :] for t in range(S5_CHUNK)], axis=1).astype(BF16)
    e = _dot(x, inject_ref[...])
    er, ei = e[:, :ns], e[:, ns:]
    h_in = h_ref[...]
    hr, hi = h_in[:, :ns], h_in[:, ns:]
    row = lax.broadcasted_iota(jnp.int32, (rows, ns), 0)
    first = row == 0
    a = scan_ref[0]
    ar, ai = a[0:1, :ns], a[0:1, ns:]
    er = er + jnp.where(first, ar * hr - ai * hi, 0.0)
    ei = ei + jnp.where(first, ar * hi + ai * hr, 0.0)
    for k in range(a.shape[0]):
        d = 1 << k
        ar, ai = a[k:k + 1, :ns], a[k:k + 1, ns:]
        keep = row >= d
        sr = jnp.where(keep, pltpu.roll(er, d, axis=0), 0.0)
        si = jnp.where(keep, pltpu.roll(ei, d, axis=0), 0.0)
        er, ei = er + ar * sr - ai * si, ei + ar * si + ai * sr
    h_ref[...] = jnp.concatenate([er[rows - 1:rows], ei[rows - 1:rows]], axis=1)
    hx_r = jnp.where(first, hr, pltpu.roll(er, 1, axis=0))
    hx_i = jnp.where(first, hi, pltpu.roll(ei, 1, axis=0))
    hx = jnp.concatenate([hx_r, hx_i], axis=1).astype(BF16)
    y = _dot(x, intra_ref[...]) + _dot(hx, readout_ref[...])
    for t in range(S5_CHUNK):
        y_ref[:, t, :] = y[:, t * LANES:(t + 1) * LANES]


def s5_ssm(u, weights, batch, seq):
    intra, inject, readout, scan = weights
    rows_per_seq = seq // S5_CHUNK
    tile = min(S5_ROWS, rows_per_seq)
    assert tile == S5_ROWS, "sequence too short for the S5 row tile"
    nrt = rows_per_seq // tile
    u3 = u.reshape(batch * rows_per_seq, S5_CHUNK, SELF_WIDTH)
    wide = S5_CHUNK * LANES
    states = 2 * S5_BLOCK_STATE
    assert wide == states, "one mask shape serves all three matrices"
    spread_tp, spread_cn, masks = _s5_spread_constants()
    compact = lambda a: pl.BlockSpec((1,) + a.shape[1:], lambda b, j, r: (j, 0, 0))
    whole = lambda a: pl.BlockSpec(a.shape, lambda b, j, r: (0,) * a.ndim)
    y3 = pl.pallas_call(
        _s5_kernel,
        grid=(batch, S5_LANE_BLOCKS, nrt),
        in_specs=[pl.BlockSpec((tile, S5_CHUNK, LANES), lambda b, j, r: (b * nrt + r, 0, j)),
                  compact(intra), compact(inject), compact(readout), compact(scan),
                  whole(spread_tp), whole(spread_cn), whole(masks)],
        out_specs=pl.BlockSpec((tile, S5_CHUNK, LANES), lambda b, j, r: (b * nrt + r, 0, j)),
        out_shape=jax.ShapeDtypeStruct(u3.shape, F32),
        scratch_shapes=[pltpu.VMEM((1, states), F32),
                        pltpu.VMEM((wide, wide), BF16),
                        pltpu.VMEM((wide, states), BF16),
                        pltpu.VMEM((states, wide), BF16)],
        compiler_params=_params(("parallel", "parallel", "arbitrary")),
        name="s5_ssm",
    )(u3, intra, inject, readout, scan, spread_tp, spread_cn, masks)
    return y3.reshape(u.shape)


def _moba_query_operand(q_tile_ref, kmh_ref, kml_ref, step, dst_ref):
    tq = MOBA_BLOCK
    q = q_tile_ref[...].astype(F32)
    lane = lax.broadcasted_iota(jnp.int32, q.shape, 1)
    row = lax.broadcasted_iota(jnp.int32, q.shape, 0)
    lo = lane < HEAD_DIM
    blk = jnp.where(lo, NO_BLOCK, (lane - HEAD_DIM).astype(F32))
    own_blk = (step * MOBA_GROUP + row // tq).astype(F32)
    past = blk < own_blk
    own = blk == own_blk
    zeros = jnp.zeros((tq, LANES), F32)
    for e in range(2):
        keep = lo if e == 0 else jnp.logical_not(lo)
        qe = jnp.where(keep, q, 0.0)
        qe16 = qe.astype(BF16)
        gate = _dot_nt(qe16, kmh_ref[0, 0]) + _dot_nt(qe16, kml_ref[0, 0])
        g = jnp.where(past, gate, -jnp.inf)
        bias = jnp.where(own, 0.0, NEG_INF)
        for _ in range(MOBA_TOPK):
            top = jnp.max(g, axis=1, keepdims=True)
            idx = jnp.min(jnp.where(g == top, blk, NO_BLOCK), axis=1, keepdims=True)
            hit = blk == idx
            bias = jnp.where(hit & past, 0.0, bias)
            g = jnp.where(hit, -jnp.inf, g)
        q_head = qe if e == 0 else pltpu.roll(qe, HEAD_DIM, axis=1)
        placed = jnp.where(lo, q_head, bias)
        for t in range(MOBA_GROUP):
            halves = [zeros, zeros]
            halves[e] = placed[t * tq:(t + 1) * tq]
            dst_ref[(2 * t + e) * tq:(2 * t + e + 1) * tq, :] = jnp.concatenate(halves, axis=1).astype(BF16)


def _moba_kernel(q_ref, q_next_ref, k_ref, vt_ref, kmh_ref, kml_ref, o_ref, qa_ref, qa_next_ref, m_ref, acc_ref,
                 sa_ref, sb_ref, ta_ref, tb_ref):
    i = pl.program_id(2)
    tq = MOBA_BLOCK
    blk_rows = MOBA_BLOCK
    nt = MOBA_GROUP
    ones = jnp.ones((MOBA_ONES_ROWS, blk_rows), BF16)

    @pl.when(i == 0)
    def _():
        _moba_query_operand(q_ref, kmh_ref, kml_ref, 0, qa_ref)

    def value_rows(n):
        return jnp.concatenate([vt_ref[n], ones], axis=0)

    grp_rows = nt * blk_rows
    last_group = k_ref.shape[1] // grp_rows - 1

    def scores(grp, dst_ref, top_ref):
        kc = k_ref[0, pl.ds(pl.multiple_of(grp * grp_rows, grp_rows), grp_rows), :]
        s = _dot_nt(kc, qa_ref[...])
        dst_ref[...] = s
        top_ref[...] = jnp.max(s, axis=0, keepdims=True)

    _moba_query_operand(q_next_ref, kmh_ref, kml_ref, i + 1, qa_next_ref)

    sb_ref[...] = _dot_nt(k_ref[0, pl.ds(pl.multiple_of(i * grp_rows, grp_rows), grp_rows), :], qa_ref[...])
    kpos = lax.broadcasted_iota(jnp.int32, (blk_rows, 2 * tq), 0)
    qpos = lax.broadcasted_iota(jnp.int32, (blk_rows, 2 * tq), 1) % tq
    causal = jnp.where(kpos <= qpos, 0.0, NEG_INF)
    for t in range(nt):
        sb_ref[t * blk_rows:(t + 1) * blk_rows, 2 * t * tq:2 * (t + 1) * tq] += causal
    m0 = jnp.max(sb_ref[...], axis=0, keepdims=True)
    m_ref[...] = m0
    acc = None
    for b in range(nt):
        p = jnp.exp2(sb_ref[b * blk_rows:(b + 1) * blk_rows, :] - m0).astype(BF16)
        part = _dot(value_rows(i * nt + b), p)
        acc = part if acc is None else acc + part
    acc_ref[...] = acc
    ngroups = i

    def absorb(grp, src_ref, top_ref):
        m_old = m_ref[...]
        m_new = jnp.maximum(m_old, top_ref[...])
        pv = None
        for i in range(MOBA_GROUP):
            p = jnp.exp2(src_ref[i * blk_rows:(i + 1) * blk_rows, :] - m_new).astype(BF16)
            part = _dot(value_rows(grp * MOBA_GROUP + i), p)
            pv = part if pv is None else pv + part
        acc_ref[...] = jnp.exp2(m_old - m_new) * acc_ref[...] + pv
        m_ref[...] = m_new

    scores(0, sa_ref, ta_ref)

    def body(t, carry):
        scores(2 * t + 1, sb_ref, tb_ref)
        absorb(2 * t, sa_ref, ta_ref)
        scores(jnp.minimum(2 * t + 2, last_group), sa_ref, ta_ref)
        absorb(2 * t + 1, sb_ref, tb_ref)
        return carry

    lax.fori_loop(0, ngroups // 2, body, 0)

    @pl.when(ngroups % 2 == 1)
    def _():
        absorb(ngroups - 1, sa_ref, ta_ref)

    acc = acc_ref[...]
    out = acc[:2 * HEAD_DIM] / acc[2 * HEAD_DIM:2 * HEAD_DIM + 1]
    for t in range(nt):
        ot = jnp.concatenate([out[:HEAD_DIM, 2 * t * tq:(2 * t + 1) * tq],
                              out[HEAD_DIM:, (2 * t + 1) * tq:(2 * t + 2) * tq]], axis=0)
        o_ref[t * tq:(t + 1) * tq, :] = ot.T.astype(o_ref.dtype)
    qa_ref[...] = qa_next_ref[...]


def moba_attention(q, kaug, vt3, kmh, kml, batch, seq):
    rows = q.shape[0]
    nb = seq // MOBA_BLOCK
    tq = MOBA_BLOCK
    assert nb % MOBA_GROUP == 0
    steps = nb // MOBA_GROUP
    q_rows = MOBA_GROUP * tq
    cols = 2 * q_rows
    acc_rows = LANES + MOBA_ONES_ROWS
    return pl.pallas_call(
        _moba_kernel,
        grid=(batch, HEAD_PAIRS, steps),
        in_specs=[pl.BlockSpec((q_rows, LANES), lambda b, h, i: (b * steps + i, h)),
                  pl.BlockSpec((q_rows, LANES), lambda b, h, i: (b * steps + jnp.minimum(i + 1, steps - 1), h)),
                  pl.BlockSpec((1, seq, 2 * LANES), lambda b, h, i: (b * HEAD_PAIRS + h, 0, 0)),
                  pl.BlockSpec((nb, LANES, MOBA_BLOCK), lambda b, h, i: (b, h, 0)),
                  pl.BlockSpec((1, 1, LANES, LANES), lambda b, h, i: (b, h, 0, 0)),
                  pl.BlockSpec((1, 1, LANES, LANES), lambda b, h, i: (b, h, 0, 0))],
        out_specs=pl.BlockSpec((q_rows, LANES), lambda b, h, i: (b * steps + i, h)),
        out_shape=jax.ShapeDtypeStruct((rows, SELF_WIDTH), BF16),
        scratch_shapes=[pltpu.VMEM((cols, 2 * LANES), BF16),
                        pltpu.VMEM((cols, 2 * LANES), BF16),
                        pltpu.VMEM((1, cols), F32),
                        pltpu.VMEM((acc_rows, cols), F32),
                        pltpu.VMEM((MOBA_GROUP * MOBA_BLOCK, cols), F32),
                        pltpu.VMEM((MOBA_GROUP * MOBA_BLOCK, cols), F32),
                        pltpu.VMEM((1, cols), F32),
                        pltpu.VMEM((1, cols), F32)],
        compiler_params=_params(("parallel", "parallel", "arbitrary")),
        name="moba_attention",
    )(q, q, kaug, vt3, kmh, kml)


def _key_mean_operand(kmean, batch, seq):
    nb = seq // MOBA_BLOCK
    km = kmean.reshape(batch, nb, HEAD_PAIRS, LANES).transpose(0, 2, 1, 3)
    km = jnp.pad(km, ((0, 0), (0, 0), (HEAD_DIM, LANES - HEAD_DIM - nb), (0, 0)))
    hi = km.astype(BF16)
    lo = (km - hi.astype(F32)).astype(BF16)
    return hi, lo


def _memory_heads(qm, km, vm):
    lane = lax.broadcasted_iota(jnp.int32, (qm.shape[0], LANES), 1)
    lo = lane < HEAD_DIM
    outs = []
    for pair in range(MEM_WIDTH // LANES):
        sl = slice(pair * LANES, (pair + 1) * LANES)
        qp, kp, vp = qm[:, sl], km[:, sl], vm[:, sl]
        zero = jnp.zeros_like(qp)
        res = []
        for keep in (lo, jnp.logical_not(lo)):
            s = _dot_nt(jnp.where(keep, qp, zero), kp) * ATTN_SCALE
            p = jnp.exp(s - jnp.max(s, axis=-1, keepdims=True))
            p = p / jnp.sum(p, axis=-1, keepdims=True)
            res.append(_dot(p.astype(BF16), vp))
        outs.append(jnp.where(lo, res[0], res[1]))
    return jnp.concatenate(outs, axis=1)


def _mixout_kernel(x_ref, s_ref, qm_ref, km_ref, vm_ref, *rest, glu):
    if glu:
        wg_ref, wo_ref, o_ref = rest
        y = jax.nn.gelu(s_ref[...])
        gate = _dot(y.astype(BF16), wg_ref[...])
        self_out = (y / (1.0 + jnp.exp(-gate))).astype(BF16)
    else:
        wo_ref, o_ref = rest
        self_out = s_ref[...]
    mem_out = _memory_heads(qm_ref[...], km_ref[...], vm_ref[...]).astype(BF16)
    mixed = jnp.concatenate([self_out, mem_out], axis=1)
    o_ref[...] = x_ref[...] + _dot(mixed, wo_ref[...])


def mix_out(x, self_pre, qm, mem_kv, w_glu, w_out, seq, tile=1024):
    rows, d = x.shape
    n_mem = mem_kv.shape[0] // (rows // seq)
    per_seq = seq // tile
    glu = w_glu is not None
    in_specs = [pl.BlockSpec((tile, d), lambda i: (i, 0)),
                pl.BlockSpec((tile, SELF_WIDTH), lambda i: (i, 0)),
                pl.BlockSpec((tile, MEM_WIDTH), lambda i: (i, 0)),
                pl.BlockSpec((n_mem, MEM_WIDTH), lambda i: (i // per_seq, 0)),
                pl.BlockSpec((n_mem, MEM_WIDTH), lambda i: (i // per_seq, 1))]
    args = [x, self_pre, qm, mem_kv, mem_kv]
    if glu:
        in_specs.append(pl.BlockSpec((SELF_WIDTH, SELF_WIDTH), lambda i: (0, 0)))
        args.append(w_glu)
    in_specs.append(pl.BlockSpec((d, d), lambda i: (0, 0)))
    args.append(w_out)
    return pl.pallas_call(
        functools.partial(_mixout_kernel, glu=glu),
        grid=(rows // tile,),
        in_specs=in_specs,
        out_specs=pl.BlockSpec((tile, d), lambda i: (i, 0)),
        out_shape=jax.ShapeDtypeStruct((rows, d), F32),
        compiler_params=_params(("parallel",)),
        name="mix_out_glu" if glu else "mix_out",
    )(*args)


def _ffn_kernel(x_ref, halo_ref, g_ref, wu_ref, cw_ref, wd_ref, *rest, per_seq, final):
    if final:
        gf_ref, o_ref, act_ref = rest
    else:
        o_ref, act_ref = rest
    x = x_ref[...]
    gain = g_ref[...]
    halo = jnp.where(pl.program_id(0) % per_seq == 0, 0.0, halo_ref[...])
    hcat = jnp.concatenate([_rms(halo, gain).astype(BF16), _rms(x, gain).astype(BF16)], axis=0)

    def conv(u, w):
        return (w[0:1] * pltpu.roll(u, 2, axis=0)[FFN_HALO:] + w[1:2] * pltpu.roll(u, 1, axis=0)[FFN_HALO:]
                + w[2:3] * u[FFN_HALO:] + w[3:4])

    for c in range(D_FF // FFN_CHUNK):
        gate_cols = slice(c * FFN_CHUNK, (c + 1) * FFN_CHUNK)
        value_cols = slice(D_FF + c * FFN_CHUNK, D_FF + (c + 1) * FFN_CHUNK)
        gg = conv(_dot(hcat, wu_ref[:, gate_cols]), cw_ref[:, gate_cols])
        vv = conv(_dot(hcat, wu_ref[:, value_cols]), cw_ref[:, value_cols])
        act_ref[:, gate_cols] = (gg / (1.0 + jnp.exp(-gg)) * vv).astype(BF16)
    out = x + _dot(act_ref[...], wd_ref[...])
    if final:
        out = _rms(out, gf_ref[...])
    o_ref[...] = out


def conv_ffn(x, gain, w_up, conv_w, conv_b, w_down, seq, final_gain=None, tile=1024):
    rows, d = x.shape
    per_seq = seq // tile
    cw = jnp.concatenate([conv_w, conv_b[None, :]], axis=0)
    final = final_gain is not None
    halo_blocks = tile // FFN_HALO
    in_specs = [pl.BlockSpec((tile, d), lambda i: (i, 0)),
                pl.BlockSpec((FFN_HALO, d), lambda i: (jnp.maximum(i * halo_blocks - 1, 0), 0)),
                pl.BlockSpec((1, d), lambda i: (0, 0)),
                pl.BlockSpec((d, 2 * D_FF), lambda i: (0, 0), pipeline_mode=pl.Buffered(1)),
                pl.BlockSpec((CONV_WIDTH + 1, 2 * D_FF), lambda i: (0, 0)),
                pl.BlockSpec((D_FF, d), lambda i: (0, 0), pipeline_mode=pl.Buffered(1))]
    args = [x, x, gain.reshape(1, d), w_up.astype(BF16), cw, w_down.astype(BF16)]
    if final:
        in_specs.append(pl.BlockSpec((1, d), lambda i: (0, 0)))
        args.append(final_gain.reshape(1, d))
    return pl.pallas_call(
        functools.partial(_ffn_kernel, per_seq=per_seq, final=final),
        grid=(rows // tile,),
        in_specs=in_specs,
        out_specs=pl.BlockSpec((tile, d), lambda i: (i, 0)),
        out_shape=jax.ShapeDtypeStruct((rows, d), F32),
        scratch_shapes=[pltpu.VMEM((tile, D_FF), BF16)],
        compiler_params=_params(("parallel",)),
        name="conv_ffn_final" if final else "conv_ffn",
    )(*args)


def _rope_tables(seq):
    pos = jnp.arange(seq, dtype=F32)
    inv = ROPE_THETA ** (-jnp.arange(0, HEAD_DIM, 2, dtype=F32) / HEAD_DIM)
    ang = pos[:, None] * inv[None, :]
    cos, sin = jnp.cos(ang), jnp.sin(ang)
    reps = LANES // HEAD_DIM
    cos_t = jnp.tile(jnp.concatenate([cos, cos], axis=1), (1, reps))
    sin_t = jnp.tile(jnp.concatenate([-sin, sin], axis=1), (1, reps))
    return cos_t, sin_t


def kernel(x, mem, ln_mix, w_in, w_out, mem_norm, w_mem_kv, ln_ffn, w_up, conv_w, conv_b, w_down,
           s5_lambda_re, s5_lambda_im, s5_log_step, s5_b_re, s5_b_im, s5_c_re, s5_c_im, s5_d,
           s5_w_glu, kv_norm, w_kv, final_norm):
    batch, seq, d = x.shape
    depth = ln_mix.shape[0]
    n_s5 = s5_lambda_re.shape[0]
    n_mem = mem.shape[1]
    xf = x.reshape(batch * seq, d)
    memf = mem.reshape(batch * n_mem, d)
    cos_t, sin_t = _rope_tables(seq)
    shared = None
    for l in range(depth):
        mem_kv = norm_matmul(memf, mem_norm[l], w_mem_kv[l].astype(BF16), n_mem, BF16)
        if l < n_s5:
            u, qm = inproj_s5(xf, ln_mix[l], w_in[l].astype(BF16))
            weights = _s5_weights(s5_lambda_re[l], s5_lambda_im[l], s5_log_step[l], s5_b_re[l], s5_b_im[l],
                                  s5_c_re[l], s5_c_im[l], s5_d[l])
            self_pre = s5_ssm(u, weights, batch, seq)
            w_glu = s5_w_glu[l].astype(BF16)
        else:
            q, qm, k3, vt3, kmean = inproj_moba(xf, ln_mix[l], kv_norm, w_in[l].astype(BF16),
                                                w_kv.astype(BF16), cos_t, sin_t, seq)
            if shared is None:
                shared = (k3, vt3, *_key_mean_operand(kmean, batch, seq))
            self_pre = moba_attention(q, *shared, batch, seq)
            w_glu = None
        xf = mix_out(xf, self_pre, qm, mem_kv, w_glu, w_out[l].astype(BF16), seq)
        xf = conv_ffn(xf, ln_ffn[l], w_up[l], conv_w[l], conv_b[l], w_down[l], seq,
                      final_gain=final_norm if l == depth - 1 else None)
    return xf.reshape(batch, seq, d)
```
